```python
import jax
import jax.numpy as jnp
from jax import lax
import numpy as np

D_MODEL = 1024
BATCH = 8
SEQ = 8192
DEPTH = 2

RW_HEADS = 8
RW_HEAD = 64
RW = RW_HEADS * RW_HEAD
RW_LORA_W = 64
RW_LORA_A = 64
RW_LORA_G = 128
RW_LORA_V = 32
RW_GN_EPS = 64e-5
CV = 512
CONV_WIDTH = 31
POOL_WINDOWS = (2, 4, 8, 16)
N_POOL = len(POOL_WINDOWS)
PW = 512
PG = PW // N_POOL
N_BRANCH = 3
D_FF = 4 * D_MODEL
N_ADA = 6
LN_EPS = 1e-5
ALPHA = (2 * DEPTH) ** 0.25
BETA = (8 * DEPTH) ** -0.25

RW_SHIFT = 3 * RW + RW_LORA_W + RW_LORA_A + RW_LORA_G
CV_OFF = RW_SHIFT
PL_OFF = CV_OFF + 2 * CV
GT_OFF = PL_OFF + PW
C_MAIN = GT_OFF + N_BRANCH * D_MODEL

kernel_name = "hybrid_rwkv7_conformer_pool_deepnorm_block"


def _layernorm(x, g, b, eps=LN_EPS):
    xf = x.astype(jnp.float32)
    mu = xf.mean(-1, keepdims=True)
    var = jnp.square(xf - mu).mean(-1, keepdims=True)
    y = (xf - mu) * lax.rsqrt(var + eps)
    return (y * g.astype(jnp.float32) + b.astype(jnp.float32)).astype(x.dtype)


def _token_shift(z, mu):
    prev = jnp.pad(z[:, :-1], ((0, 0), (1, 0), (0, 0)))
    return z + (prev - z) * mu


def _rwkv7_recurrence(r, w, k, v, a, b):
    Bn, S, H, N = r.shape

    def step(state, inp):
        r_t, w_t, k_t, v_t, a_t, b_t = inp
        sa = jnp.einsum('bhvk,bhk->bhv', state, a_t)
        state = (state * w_t[:, :, None, :] + sa[..., None] * b_t[:, :, None, :]
                 + v_t[..., None] * k_t[:, :, None, :])
        y_t = jnp.einsum('bhvk,bhk->bhv', state, r_t)
        return state, y_t

    s0 = jnp.zeros((Bn, H, N, N), jnp.float32)
    xs = tuple(jnp.moveaxis(t, 1, 0) for t in (r, w, k, v, a, b))
    _, y = lax.scan(step, s0, xs)
    return jnp.moveaxis(y, 0, 1)


def _rwkv7_branch(z, lv, v_first, w0, w2, a0, a2, g2, v0, v2, k_k, k_a, r_k, gn_g, gn_b, w_o):
    Bn, S, _ = z.shape
    f32 = jnp.float32
    r, k, v, lw, la, lg = jnp.split(
        z, [RW, 2 * RW, 3 * RW, 3 * RW + RW_LORA_W, 3 * RW + RW_LORA_W + RW_LORA_A], axis=-1)
    w_log = -jax.nn.softplus(-(w0 + jnp.tanh(lw) @ w2).astype(f32)) - 0.5
    decay = jnp.exp(-jnp.exp(w_log))
    a = jax.nn.sigmoid(a0 + la @ a2)
    g = jax.nn.sigmoid(lg) @ g2
    if v_first is None:
        v_first = v
    else:
        v = v + (v_first - v) * jax.nn.sigmoid(v0 + lv @ v2)

    def heads(t):
        return t.astype(f32).reshape(Bn, S, RW_HEADS, RW_HEAD)

    kk = heads(k * k_k)
    kk = kk / jnp.maximum(jnp.sqrt(jnp.sum(kk * kk, -1, keepdims=True)), 1e-12)
    a_h = heads(a)
    k_h = heads(k * (1 + (a - 1) * k_a))
    r_h = heads(r)
    v_h = heads(v)
    y = _rwkv7_recurrence(r_h, decay.reshape(Bn, S, RW_HEADS, RW_HEAD), k_h, v_h, -kk, kk * a_h)
    mu = y.mean(-1, keepdims=True)
    var = jnp.square(y - mu).mean(-1, keepdims=True)
    y = (y - mu) * lax.rsqrt(var + RW_GN_EPS)
    y = y.reshape(Bn, S, RW) * gn_g.astype(f32) + gn_b.astype(f32)
    bonus = jnp.sum(r_h * k_h * r_k.astype(f32), -1, keepdims=True) * v_h
    y = (y + bonus.reshape(Bn, S, RW)).astype(z.dtype) * g
    return y @ w_o, v_first


def _conv_branch(u, conv_w, conv_b, ln_g, ln_b, w_o):
    val, gate = jnp.split(u, 2, axis=-1)
    h = val * jax.nn.sigmoid(gate)
    h = lax.conv_general_dilated(
        h, conv_w[:, None, :], window_strides=(1,),
        padding=((CONV_WIDTH - 1, 0),),
        dimension_numbers=('NWC', 'WIO', 'NWC'),
        feature_group_count=CV) + conv_b
    h = jax.nn.silu(_layernorm(h, ln_g, ln_b))
    return h @ w_o


def _pool_branch(p, lin_w, scale, w_o):
    Bn, S, _ = p.shape
    pf = p.astype(jnp.float32)
    cs = jnp.cumsum(pf, axis=1)
    t1 = jnp.arange(1, S + 1, dtype=jnp.float32)[None, :, None]
    outs = []
    for gi, win in enumerate(POOL_WINDOWS):
        sl = slice(gi * PG, (gi + 1) * PG)
        csg = cs[..., sl]
        lower = jnp.pad(csg[:, :S - win], ((0, 0), (win, 0), (0, 0)))
        mean = (csg - lower) / jnp.minimum(t1, float(win))
        outs.append(mean - pf[..., sl])
    pooled = jnp.stack(outs, axis=2).astype(p.dtype)
    mixed = jnp.einsum('bsgc,gcd->bsgd', pooled, lin_w).reshape(Bn, S, PW)
    return (mixed * scale) @ w_o


def setup_inputs(seed: int = 0) -> dict:
    key = jax.random.key(seed)
    ks = iter(jax.random.split(key, 40))
    f32 = jnp.float32

    def nrm(shape, s):
        return jax.random.normal(next(ks), shape, f32) * s

    def unif(shape, lo, hi):
        return jax.random.uniform(next(ks), shape, f32, lo, hi)

    L, D, Lv = DEPTH, D_MODEL, DEPTH - 1
    return {
        "x": nrm((BATCH, SEQ, D), 1.0),
        "c": nrm((BATCH, D), 1.0),
        "ada_w": nrm((L, D, N_ADA * D), 0.5 * D ** -0.5),
        "ada_b": nrm((L, N_ADA * D), 0.02),
        "w_in": nrm((L, D, C_MAIN), D ** -0.5),
        "w_in_vres": nrm((Lv, D, RW_LORA_V), D ** -0.5),
        "shift_mu": unif((L, RW_SHIFT), 0.0, 1.0),
        "shift_mu_vres": unif((Lv, RW_LORA_V), 0.0, 1.0),
        "rw_w0": unif((L, RW), -5.0, -1.0),
        "rw_w2": nrm((L, RW_LORA_W, RW), 0.5 * RW_LORA_W ** -0.5),
        "rw_a0": nrm((L, RW), 0.1),
        "rw_a2": nrm((L, RW_LORA_A, RW), 0.5 * RW_LORA_A ** -0.5),
        "rw_g2": nrm((L, RW_LORA_G, RW), RW_LORA_G ** -0.5),
        "rw_v0": 1.0 + nrm((Lv, RW), 0.1),
        "rw_v2": nrm((Lv, RW_LORA_V, RW), 0.5 * RW_LORA_V ** -0.5),
        "rw_kk": 0.85 + nrm((L, RW), 0.05),
        "rw_ka": 1.0 + nrm((L, RW), 0.05),
        "rw_rk": nrm((L, RW_HEADS, RW_HEAD), 0.1),
        "rw_gn_g": 1.0 + nrm((L, RW), 0.05),
        "rw_gn_b": nrm((L, RW), 0.02),
        "rw_wo": nrm((L, RW, D), BETA * RW ** -0.5),
        "cv_w": nrm((L, CONV_WIDTH, CV), CONV_WIDTH ** -0.5),
        "cv_b": nrm((L, CV), 0.02),
        "cv_ln_g": 1.0 + nrm((L, CV), 0.05),
        "cv_ln_b": nrm((L, CV), 0.02),
        "cv_wo": nrm((L, CV, D), BETA * CV ** -0.5),
        "pl_w": nrm((L, N_POOL, PG, PG), PG ** -0.5),
        "pl_scale": 1.0 + nrm((L, PW), 0.1),
        "pl_wo": nrm((L, PW, D), BETA * PW ** -0.5),
        "w_out": nrm((L, D, D), BETA * D ** -0.5),
        "ln_m_g": 1.0 + nrm((L, D), 0.05),
        "ln_m_b": nrm((L, D), 0.02),
        "mlp_w1": nrm((L, D, D_FF), BETA * D ** -0.5),
        "mlp_w2": nrm((L, D_FF, D), BETA * D_FF ** -0.5),
        "ln_f_g": 1.0 + nrm((L, D), 0.05),
        "ln_f_b": nrm((L, D), 0.02),
    }


def reference(x, c, ada_w, ada_b, w_in, w_in_vres, shift_mu, shift_mu_vres,
              rw_w0, rw_w2, rw_a0, rw_a2, rw_g2, rw_v0, rw_v2, rw_kk, rw_ka, rw_rk,
              rw_gn_g, rw_gn_b, rw_wo, cv_w, cv_b, cv_ln_g, cv_ln_b, cv_wo,
              pl_w, pl_scale, pl_wo, w_out, ln_m_g, ln_m_b, mlp_w1, mlp_w2,
              ln_f_g, ln_f_b):
    Bn, S, D = x.shape
    cond = jax.nn.silu(c)
    v_first = None
    for l in range(DEPTH):
        ada = (cond @ ada_w[l] + ada_b[l])[:, None, :]
        sh_m, sc_m, gt_m, sh_f, sc_f, gt_f = jnp.split(ada, N_ADA, axis=-1)

        h = x * (1 + sc_m) + sh_m
        if l == 0:
            proj = h @ w_in[l]
            lv = None
        else:
            proj = h @ jnp.concatenate([w_in[l], w_in_vres[l - 1]], axis=1)
            lv = _token_shift(proj[..., C_MAIN:], shift_mu_vres[l - 1])
        z = _token_shift(proj[..., :RW_SHIFT], shift_mu[l])
        u = proj[..., CV_OFF:PL_OFF]
        p = proj[..., PL_OFF:GT_OFF]
        gates = jax.nn.sigmoid(proj[..., GT_OFF:C_MAIN]).reshape(Bn, S, N_BRANCH, D)

        y_rw, v_first = _rwkv7_branch(
            z, lv, v_first, rw_w0[l], rw_w2[l], rw_a0[l], rw_a2[l], rw_g2[l],
            rw_v0[l - 1] if l > 0 else None, rw_v2[l - 1] if l > 0 else None,
            rw_kk[l], rw_ka[l], rw_rk[l], rw_gn_g[l], rw_gn_b[l], rw_wo[l])
        y_cv = _conv_branch(u, cv_w[l], cv_b[l], cv_ln_g[l], cv_ln_b[l], cv_wo[l])
        y_pl = _pool_branch(p, pl_w[l], pl_scale[l], pl_wo[l])
        merged = gates[:, :, 0] * y_rw + gates[:, :, 1] * y_cv + gates[:, :, 2] * y_pl
        x = _layernorm(ALPHA * x + gt_m * (merged @ w_out[l]), ln_m_g[l], ln_m_b[l])

        h = x * (1 + sc_f) + sh_f
        y_ff = jnp.square(jax.nn.relu(h @ mlp_w1[l])) @ mlp_w2[l]
        x = _layernorm(ALPHA * x + gt_f * y_ff, ln_f_g[l], ln_f_b[l])
    return x
```

```python
import functools
import math

import jax
import jax.numpy as jnp
from jax import lax
from jax.experimental import pallas as pl
from jax.experimental.pallas import tpu as pltpu

F32 = jnp.float32
BF16 = jnp.bfloat16

LN_EPS = 1e-5
RW_GN_EPS = 64e-5
POOL_WINDOWS = (2, 4, 8, 16)
KK_NORM_FLOOR = 1e-12
DECAY_SCALE = math.exp(-0.5)

LANES = 128
SUBLANES = 8
VMEM_LIMIT_BYTES = 56 * 1024 * 1024

CHUNK = 64
HEAD_GROUP = 4


def _dot(a, b):
    return jnp.dot(a.astype(BF16), b.astype(BF16), preferred_element_type=F32)


def _dot_nt(a, b):
    return lax.dot_general(a.astype(BF16), b.astype(BF16), (((1,), (1,)), ((), ())),
                           preferred_element_type=F32)


def _dot_tn(a, b):
    return lax.dot_general(a.astype(BF16), b.astype(BF16), (((0,), (0,)), ((), ())),
                           preferred_element_type=F32)


def _split3(x):
    hi = x.astype(BF16)
    r1 = x - hi.astype(F32)
    mid = r1.astype(BF16)
    lo = (r1 - mid.astype(F32)).astype(BF16)
    return hi, mid, lo


def _layernorm(x, g, b, eps):
    mu = jnp.mean(x, axis=-1, keepdims=True)
    d = x - mu
    var = jnp.mean(d * d, axis=-1, keepdims=True)
    return d * lax.rsqrt(var + eps) * g + b


def _sigmoid(x):
    return 1.0 / (1.0 + jnp.exp(-x))


def _ada_kernel(c_ref, w_ref, b_ref, o_ref):
    c = c_ref[...]
    cond = c * _sigmoid(c)
    c_hi, c_mid, c_lo = _split3(cond)
    w_hi, w_mid, w_lo = _split3(w_ref[...])
    dd = lambda a, b: jnp.dot(a, b, preferred_element_type=F32)
    acc = dd(c_hi, w_hi) + (dd(c_hi, w_mid) + dd(c_mid, w_hi)) + (dd(c_hi, w_lo) + dd(c_mid, w_mid) + dd(c_lo, w_hi))
    o_ref[...] = acc + b_ref[...]


def _ada_call(c, ada_w, ada_b):
    depth, d, nd = ada_w.shape
    n_ada = nd // d
    bsz = c.shape[0]
    out = pl.pallas_call(
        _ada_kernel,
        grid=(depth, n_ada),
        in_specs=[
            pl.BlockSpec((bsz, d), lambda l, j: (0, 0)),
            pl.BlockSpec((None, d, d), lambda l, j: (l, 0, j)),
            pl.BlockSpec((None, None, 1, d), lambda l, j: (l, j, 0, 0)),
        ],
        out_specs=pl.BlockSpec((None, None, bsz, d), lambda l, j: (l, j, 0, 0)),
        out_shape=jax.ShapeDtypeStruct((depth, n_ada, bsz, d), F32),
        name="ada",
    )(c, ada_w, ada_b.reshape(depth, n_ada, 1, d))
    return jnp.transpose(out, (0, 2, 1, 3))


def _inproj_kernel(x_ref, ada_ref, wz_ref, wu_ref, wp_ref, wg_ref, z_ref, u_ref, p_ref, g_ref):
    sh = ada_ref[0:1, :]
    sc = ada_ref[1:2, :]
    h = (x_ref[...] * (1.0 + sc) + sh).astype(BF16)
    for w_ref, o_ref in ((wz_ref, z_ref), (wu_ref, u_ref), (wp_ref, p_ref), (wg_ref, g_ref)):
        o_ref[...] = jnp.dot(h, w_ref[...], preferred_element_type=F32).astype(o_ref.dtype)


def _inproj_call(x, ada, wz, wu, wp, wg, tm):
    bsz, s, d = x.shape
    widths = [w.shape[1] for w in (wz, wu, wp, wg)]
    const = lambda b, i: (0, 0)
    return pl.pallas_call(
        _inproj_kernel,
        grid=(bsz, s // tm),
        in_specs=[
            pl.BlockSpec((None, tm, d), lambda b, i: (b, i, 0)),
            pl.BlockSpec((None,) + ada.shape[1:], lambda b, i: (b, 0, 0)),
        ] + [pl.BlockSpec((d, w), const, pipeline_mode=pl.Buffered(1)) for w in widths],
        out_specs=[pl.BlockSpec((None, tm, w), lambda b, i: (b, i, 0)) for w in widths],
        out_shape=[jax.ShapeDtypeStruct((bsz, s, w), BF16) for w in widths],
        compiler_params=pltpu.CompilerParams(
            dimension_semantics=("parallel", "arbitrary"), vmem_limit_bytes=VMEM_LIMIT_BYTES),
        name="inproj",
    )(x, ada, wz, wu, wp, wg)


def _cvpl_kernel(u_ref, p_ref, g1_ref, g2_ref, cvw_ref, cvb_ref, lng_ref, lnb_ref, cvwo_ref,
                 plw_ref, pls_ref, plwo_ref, o_ref, hbuf, pbuf, *, tm, cv, conv_width, pg):
    i = pl.program_id(1)
    halo_c = hbuf.shape[0] - tm
    halo_p = pbuf.shape[0] - tm

    @pl.when(i == 0)
    def _():
        hbuf[0:halo_c, :] = jnp.zeros((halo_c, cv), F32)
        pbuf[0:halo_p, :] = jnp.zeros((halo_p, pbuf.shape[1]), F32)

    u = u_ref[...].astype(F32)
    hbuf[halo_c:halo_c + tm, :] = u[:, :cv] * _sigmoid(u[:, cv:])
    base = halo_c - (conv_width - 1)
    acc = hbuf[base:base + tm, :] * cvw_ref[0:1, :]
    for j in range(1, conv_width):
        acc = acc + hbuf[base + j:base + j + tm, :] * cvw_ref[j:j + 1, :]
    acc = acc + cvb_ref[...]
    hn = _layernorm(acc, lng_ref[...], lnb_ref[...], LN_EPS)
    hn = hn * _sigmoid(hn)
    y_cv = _dot(hn, cvwo_ref[...])
    hbuf[0:halo_c, :] = hbuf[tm:tm + halo_c, :]

    pf = p_ref[...].astype(F32)
    pbuf[halo_p:halo_p + tm, :] = pf
    t1 = (lax.broadcasted_iota(jnp.int32, (tm, 1), 0) + (i * tm + 1)).astype(F32)
    mixed = []
    for gi, win in enumerate(POOL_WINDOWS):
        sl = slice(gi * pg, (gi + 1) * pg)
        ssum = pbuf[halo_p:halo_p + tm, sl]
        for d in range(1, win):
            ssum = ssum + pbuf[halo_p - d:halo_p - d + tm, sl]
        pooled = ssum / jnp.minimum(t1, float(win)) - pf[:, sl]
        mixed.append(_dot(pooled, plw_ref[gi]))
    mixed = jnp.concatenate(mixed, axis=1) * pls_ref[...]
    y_pl = _dot(mixed, plwo_ref[...])
    pbuf[0:halo_p, :] = pbuf[tm:tm + halo_p, :]

    o_ref[...] = (_sigmoid(g1_ref[...].astype(F32)) * y_cv
                  + _sigmoid(g2_ref[...].astype(F32)) * y_pl)


def _cvpl_call(u, p, g, cvw, cvb, lng, lnb, cvwo, plw, pls, plwo, tm):
    bsz, s, _ = u.shape
    conv_width, cv = cvw.shape
    pw = p.shape[-1]
    d = cvwo.shape[1]
    pg = plw.shape[1]
    halo_c = -(-(conv_width - 1) // SUBLANES) * SUBLANES
    halo_p = -(-(max(POOL_WINDOWS) - 1) // SUBLANES) * SUBLANES
    c2 = lambda b, i: (0, 0)
    kern = functools.partial(_cvpl_kernel, tm=tm, cv=cv, conv_width=conv_width, pg=pg)
    return pl.pallas_call(
        kern,
        grid=(bsz, s // tm),
        in_specs=[
            pl.BlockSpec((None, tm, 2 * cv), lambda b, i: (b, i, 0)),
            pl.BlockSpec((None, tm, pw), lambda b, i: (b, i, 0)),
            pl.BlockSpec((None, tm, d), lambda b, i: (b, i, 1)),
            pl.BlockSpec((None, tm, d), lambda b, i: (b, i, 2)),
            pl.BlockSpec(cvw.shape, c2),
            pl.BlockSpec((1, cv), c2),
            pl.BlockSpec((1, cv), c2),
            pl.BlockSpec((1, cv), c2),
            pl.BlockSpec(cvwo.shape, c2),
            pl.BlockSpec(plw.shape, lambda b, i: (0, 0, 0)),
            pl.BlockSpec((1, pw), c2),
            pl.BlockSpec(plwo.shape, c2),
        ],
        out_specs=pl.BlockSpec((None, tm, d), lambda b, i: (b, i, 0)),
        out_shape=jax.ShapeDtypeStruct((bsz, s, d), F32),
        scratch_shapes=[pltpu.VMEM((tm + halo_c, cv), F32), pltpu.VMEM((tm + halo_p, pw), F32)],
        compiler_params=pltpu.CompilerParams(
            dimension_semantics=("parallel", "arbitrary"), vmem_limit_bytes=VMEM_LIMIT_BYTES),
        name="cvpl",
    )(u, p, g, g, cvw, cvb.reshape(1, cv), lng.reshape(1, cv), lnb.reshape(1, cv), cvwo,
      plw, pls.reshape(1, pw), plwo)


def _head_sum(x, head):
    per_slab = LANES // head
    lane = lax.broadcasted_iota(jnp.int32, (1, LANES), 1)
    out = []
    for s0 in range(0, x.shape[1], LANES):
        slab = x[:, s0:s0 + LANES]
        res = jnp.zeros_like(slab)
        for q in range(per_slab):
            m = (lane >= q * head) & (lane < (q + 1) * head)
            tot = jnp.sum(jnp.where(m, slab, 0.0), axis=-1, keepdims=True)
            res = jnp.where(m, tot, res)
        out.append(res)
    return jnp.concatenate(out, axis=1)


def _stack_heads(x, head):
    lane = lax.broadcasted_iota(jnp.int32, (1, x.shape[1]), 1)
    blocks = []
    for h in range(x.shape[1] // head):
        m = (lane >= h * head) & (lane < (h + 1) * head)
        blocks.append(jnp.where(m, x, 0.0))
    return jnp.concatenate(blocks, axis=0)


def _unstack_heads(xs, rows):
    out = xs[0:rows]
    for h in range(1, xs.shape[0] // rows):
        out = out + xs[h * rows:(h + 1) * rows]
    return out


def _rwkv_kernel(*refs, tm, rw, head, has_vres, alpha):
    if has_vres:
        (z_ref, g0_ref, m12_ref, x_ref, ada_ref, vf_ref, mu_ref, wa2_ref, g2_ref, v2_ref, vec_ref, wo_ref,
         wout_ref, lnv_ref, o_ref,
         zbuf, r_s, lw_s, k_s, v_s, a_s, b_s, g_s, bonus_s, y_s, h_s) = refs
        vfo_ref = None
    else:
        (z_ref, g0_ref, m12_ref, x_ref, ada_ref, mu_ref, wa2_ref, g2_ref, vec_ref, wo_ref,
         wout_ref, lnv_ref, o_ref, vfo_ref,
         zbuf, r_s, lw_s, k_s, v_s, a_s, b_s, g_s, bonus_s, y_s, h_s) = refs
        vf_ref = v2_ref = None
    i = pl.program_id(1)
    gw = HEAD_GROUP * head
    n_grp = rw // gw

    @pl.when(i == 0)
    def _():
        zbuf[0:SUBLANES, :] = jnp.zeros((SUBLANES, zbuf.shape[1]), F32)
        h_s[...] = jnp.zeros(h_s.shape, F32)

    zr = z_ref[...].astype(F32)
    zbuf[SUBLANES:SUBLANES + tm, :] = zr
    prev = zbuf[SUBLANES - 1:SUBLANES - 1 + tm, :]
    z = zr + (prev - zr) * mu_ref[...]
    zbuf[0:SUBLANES, :] = zbuf[tm:tm + SUBLANES, :]

    r = z[:, 0:rw]
    k = z[:, rw:2 * rw]
    v = z[:, 2 * rw:3 * rw]
    lwa = z[:, 3 * rw:3 * rw + LANES]
    lg = z[:, 3 * rw + LANES:3 * rw + 2 * LANES]
    w0, a0, k_k, k_a, r_k, gn_g, gn_b, v0 = (vec_ref[j:j + 1, :] for j in range(8))

    lane = lax.broadcasted_iota(jnp.int32, (1, LANES), 1)
    lora_in = jnp.where(lane < LANES // 2, jnp.tanh(lwa), lwa)
    wl = _dot(lora_in, wa2_ref[...])
    lwd = -DECAY_SCALE * _sigmoid(w0 + wl[:, 0:rw])
    a = _sigmoid(a0 + wl[:, rw:2 * rw])
    g_s[...] = _dot(_sigmoid(lg), g2_ref[...])
    if has_vres:
        lv = z[:, 3 * rw + 2 * LANES:3 * rw + 3 * LANES]
        v = v + (vf_ref[...] - v) * _sigmoid(v0 + _dot(lv, v2_ref[...]))
    else:
        vfo_ref[...] = v
    kk = k * k_k
    kkn = kk / jnp.maximum(jnp.sqrt(_head_sum(kk * kk, head)), KK_NORM_FLOOR)
    kp = k * (1.0 + (a - 1.0) * k_a)
    bonus_s[...] = _head_sum(r * kp * r_k, head) * v
    r_s[...] = r
    lw_s[...] = lwd
    k_s[...] = kp
    v_s[...] = v
    a_s[...] = -kkn
    b_s[...] = kkn * a

    L = CHUNK
    ri = lax.broadcasted_iota(jnp.int32, (L, L), 0)
    ci = lax.broadcasted_iota(jnp.int32, (L, L), 1)
    tri = jnp.where(ri >= ci, 1.0, 0.0).astype(BF16)
    gr = lax.broadcasted_iota(jnp.int32, (gw, gw), 0)
    gc = lax.broadcasted_iota(jnp.int32, (gw, gw), 1)
    same = (gr // L) == (gc // L)
    mask_s = same & ((gr % L) > (gc % L))
    mask_i = same & ((gr % L) >= (gc % L))
    eye = gr == gc
    mid = L // 2 - 1

    def chunk_body(c, carry):
        rows = pl.ds(pl.multiple_of(c * L, L), L)
        lw_c = lw_s[rows, :]
        hi, md, lo = _split3(lw_c)
        dd = lambda a_, b_: jnp.dot(a_, b_, preferred_element_type=F32)
        cl = dd(tri, hi) + dd(tri, md) + dd(tri, lo)
        c_mid = cl[mid:mid + 1, :]
        c_end = cl[L - 1:L, :]
        w_inc = jnp.exp(cl - c_mid)
        w_exc = jnp.exp(cl - lw_c - c_mid)
        w_inv = jnp.exp(c_mid - cl)
        w_end = jnp.exp(c_end - cl)
        e0 = jnp.exp(c_mid)
        wl_end = jnp.exp(c_end)
        r_c, k_c, v_c, a_c, b_c = r_s[rows, :], k_s[rows, :], v_s[rows, :], a_s[rows, :], b_s[rows, :]
        for gi in range(n_grp):
            ls = slice(gi * gw, (gi + 1) * gw)
            a_st = _stack_heads(a_c[:, ls] * w_exc[:, ls], head)
            r_st = _stack_heads(r_c[:, ls] * w_inc[:, ls], head)
            v_st = _stack_heads(v_c[:, ls], head)
            b_t = b_c[:, ls] * w_inv[:, ls]
            k_t = k_c[:, ls] * w_inv[:, ls]
            bh_st = _stack_heads(b_c[:, ls] * w_end[:, ls], head)
            kh_st = _stack_heads(k_c[:, ls] * w_end[:, ls], head)
            lhs = jnp.concatenate([a_st, r_st], axis=0)
            rhs = jnp.concatenate([b_t] * HEAD_GROUP + [k_t] * HEAD_GROUP, axis=0)
            s1 = _dot_nt(lhs, rhs)
            m_ab = jnp.where(mask_s, s1[0:gw, 0:gw], 0.0)
            m_ak = jnp.where(mask_s, s1[0:gw, gw:2 * gw], 0.0)
            m_rb = jnp.where(mask_i, s1[gw:2 * gw, 0:gw], 0.0)
            m_rk = jnp.where(mask_i, s1[gw:2 * gw, gw:2 * gw], 0.0)
            xs = jnp.concatenate([a_st, _dot(m_ak, v_st)], axis=1)
            pw_ = m_ab
            n_dbl = int(math.log2(L))
            for it in range(n_dbl):
                xs = xs + _dot(pw_, xs)
                if it < n_dbl - 1:
                    pw_ = _dot(pw_, pw_)
            w1 = _dot(m_rb, xs)
            rhat = _unstack_heads(r_st + w1[:, 0:gw], L) * e0[:, ls]
            y0 = _unstack_heads(w1[:, gw:2 * gw] + _dot(m_rk, v_st), L)
            pq = _dot_tn(bh_st, xs)
            p_bd = jnp.where(eye, wl_end[:, ls], 0.0) + pq[:, 0:gw] * e0[:, ls]
            q_bd = pq[:, gw:2 * gw] + _dot_tn(kh_st, v_st)
            h_in = h_s[gi]
            y_s[rows, ls] = _dot(rhat, h_in) + y0
            h_s[gi] = _dot(p_bd, h_in) + q_bd
        return carry

    lax.fori_loop(0, tm // L, chunk_body, 0)

    y = y_s[...]
    mu = _head_sum(y, head) * (1.0 / head)
    dlt = y - mu
    var = _head_sum(dlt * dlt, head) * (1.0 / head)
    yn = dlt * lax.rsqrt(var + RW_GN_EPS) * gn_g + gn_b
    y_rw = _dot((yn + bonus_s[...]) * g_s[...], wo_ref[...])
    merged = _sigmoid(g0_ref[...].astype(F32)) * y_rw + m12_ref[...]
    o = _dot(merged, wout_ref[...])
    gt = ada_ref[2:3, :]
    o_ref[...] = _layernorm(alpha * x_ref[...] + gt * o, lnv_ref[0:1, :], lnv_ref[1:2, :], LN_EPS)


def _rwkv_call(z, g, m12, x, ada, vfirst, mu, wa2, g2, v2, vecs, wo, wout, lnv, tm, head, alpha):
    bsz, s, d = x.shape
    zw = z.shape[-1]
    rw = wo.shape[0]
    has_vres = vfirst is not None
    c2 = lambda b, i: (0, 0)
    tile = lambda w: pl.BlockSpec((None, tm, w), lambda b, i: (b, i, 0))
    in_specs = [tile(zw), tile(d), tile(d), tile(d),
                pl.BlockSpec((None,) + ada.shape[1:], lambda b, i: (b, 0, 0))]
    args = [z, g, m12, x, ada]
    if has_vres:
        in_specs.append(tile(rw))
        args.append(vfirst)
    in_specs += [pl.BlockSpec(mu.shape, c2), pl.BlockSpec(wa2.shape, c2), pl.BlockSpec(g2.shape, c2)]
    args += [mu, wa2, g2]
    if has_vres:
        in_specs.append(pl.BlockSpec(v2.shape, c2))
        args.append(v2)
    in_specs += [pl.BlockSpec(vecs.shape, c2), pl.BlockSpec(wo.shape, c2), pl.BlockSpec(wout.shape, c2),
                 pl.BlockSpec(lnv.shape, c2)]
    args += [vecs, wo, wout, lnv]
    out_specs = [tile(d)]
    out_shape = [jax.ShapeDtypeStruct((bsz, s, d), F32)]
    if not has_vres:
        out_specs.append(tile(rw))
        out_shape.append(jax.ShapeDtypeStruct((bsz, s, rw), F32))
    gw = HEAD_GROUP * head
    scratch = [pltpu.VMEM((tm + SUBLANES, zw), F32)]
    scratch += [pltpu.VMEM((tm, rw), F32) for _ in range(9)]
    scratch += [pltpu.VMEM((rw // gw, gw, gw), F32)]
    kern = functools.partial(_rwkv_kernel, tm=tm, rw=rw, head=head, has_vres=has_vres, alpha=alpha)
    res = pl.pallas_call(
        kern,
        grid=(bsz, s // tm),
        in_specs=in_specs,
        out_specs=out_specs,
        out_shape=out_shape,
        scratch_shapes=scratch,
        compiler_params=pltpu.CompilerParams(
            dimension_semantics=("parallel", "arbitrary"), vmem_limit_bytes=VMEM_LIMIT_BYTES),
        name="rwkv_vres" if has_vres else "rwkv",
    )(*args)
    return (res[0], vfirst) if has_vres else (res[0], res[1])


def _mlp_kernel(x_ref, ada_ref, w1_ref, w2_ref, lnv_ref, o_ref, *, alpha, n_split):
    x = x_ref[...]
    sh = ada_ref[3:4, :]
    sc = ada_ref[4:5, :]
    gt = ada_ref[5:6, :]
    h = (x * (1.0 + sc) + sh).astype(BF16)
    dff = w1_ref.shape[1]
    step = dff // n_split
    acc = None
    for j in range(n_split):
        t = jnp.dot(h, w1_ref[:, j * step:(j + 1) * step], preferred_element_type=F32)
        t = jnp.square(jnp.maximum(t, 0.0)).astype(BF16)
        part = jnp.dot(t, w2_ref[j * step:(j + 1) * step, :], preferred_element_type=F32)
        acc = part if acc is None else acc + part
    o_ref[...] = _layernorm(alpha * x + gt * acc, lnv_ref[0:1, :], lnv_ref[1:2, :], LN_EPS)


def _mlp_call(x, ada, w1, w2, lnv, tm, alpha):
    bsz, s, d = x.shape
    dff = w1.shape[1]
    c2 = lambda b, i: (0, 0)
    kern = functools.partial(_mlp_kernel, alpha=alpha, n_split=dff // d)
    return pl.pallas_call(
        kern,
        grid=(bsz, s // tm),
        in_specs=[
            pl.BlockSpec((None, tm, d), lambda b, i: (b, i, 0)),
            pl.BlockSpec((None,) + ada.shape[1:], lambda b, i: (b, 0, 0)),
            pl.BlockSpec(w1.shape, c2, pipeline_mode=pl.Buffered(1)),
            pl.BlockSpec(w2.shape, c2, pipeline_mode=pl.Buffered(1)),
            pl.BlockSpec(lnv.shape, c2),
        ],
        out_specs=pl.BlockSpec((None, tm, d), lambda b, i: (b, i, 0)),
        out_shape=jax.ShapeDtypeStruct((bsz, s, d), F32),
        compiler_params=pltpu.CompilerParams(
            dimension_semantics=("parallel", "arbitrary"), vmem_limit_bytes=VMEM_LIMIT_BYTES),
        name="mlp",
    )(x, ada, w1, w2, lnv)


def _pick_tile(s, pref):
    tm = min(pref, s)
    assert s % tm == 0 and tm % CHUNK == 0, (s, tm)
    return tm


def kernel(x, c, ada_w, ada_b, w_in, w_in_vres, shift_mu, shift_mu_vres, rw_w0, rw_w2, rw_a0, rw_a2, rw_g2, rw_v0, rw_v2, rw_kk, rw_ka, rw_rk, rw_gn_g, rw_gn_b, rw_wo, cv_w, cv_b, cv_ln_g, cv_ln_b, cv_wo, pl_w, pl_scale, pl_wo, w_out, ln_m_g, ln_m_b, mlp_w1, mlp_w2, ln_f_g, ln_f_b):
    bsz, s, d = x.shape
    depth = ada_w.shape[0]
    heads, head = rw_rk.shape[1], rw_rk.shape[2]
    rw = heads * head
    lw_w, la_w, lg_w = rw_w2.shape[1], rw_a2.shape[1], rw_g2.shape[1]
    lv_w = rw_v2.shape[1]
    cv = cv_w.shape[2]
    pw = pl_scale.shape[1]
    assert lw_w + la_w == LANES and lg_w == LANES and lv_w <= LANES and LANES % head == 0
    assert rw % (HEAD_GROUP * head) == 0 and head == CHUNK
    alpha = float((2 * depth) ** 0.25)
    rw_shift = 3 * rw + lw_w + la_w + lg_w
    cv_off, pl_off = rw_shift, rw_shift + 2 * cv
    gt_off = pl_off + pw

    tm_in = _pick_tile(s, 512)
    tm_cv = _pick_tile(s, 512)
    tm_rw = _pick_tile(s, 256)
    tm_ff = _pick_tile(s, 512)

    ada = _ada_call(c, ada_w, ada_b)

    v_first = None
    for l in range(depth):
        has_vres = l > 0
        w_l = w_in[l]
        wz = w_l[:, :rw_shift]
        mu = shift_mu[l]
        if has_vres:
            pad = LANES - lv_w
            wz = jnp.concatenate([wz, w_in_vres[l - 1], jnp.zeros((d, pad), F32)], axis=1)
            mu = jnp.concatenate([mu, shift_mu_vres[l - 1], jnp.zeros((pad,), F32)])
        wz, wu, wp, wg = (t.astype(BF16) for t in (wz, w_l[:, cv_off:pl_off], w_l[:, pl_off:gt_off], w_l[:, gt_off:]))
        z, u, p, g = _inproj_call(x, ada[l], wz, wu, wp, wg, tm_in)

        m12 = _cvpl_call(u, p, g, cv_w[l], cv_b[l], cv_ln_g[l], cv_ln_b[l], cv_wo[l].astype(BF16),
                         pl_w[l].astype(BF16), pl_scale[l], pl_wo[l].astype(BF16), tm_cv)

        wa2 = jnp.zeros((LANES, 2 * rw), F32)
        wa2 = wa2.at[:lw_w, :rw].set(rw_w2[l]).at[lw_w:, rw:].set(rw_a2[l]).astype(BF16)
        v0 = rw_v0[l - 1] if has_vres else jnp.zeros((rw,), F32)
        vecs = jnp.stack([rw_w0[l], rw_a0[l], rw_kk[l], rw_ka[l], rw_rk[l].reshape(rw),
                          rw_gn_g[l], rw_gn_b[l], v0])
        v2 = None
        if has_vres:
            v2 = jnp.zeros((LANES, rw), F32).at[:lv_w].set(rw_v2[l - 1]).astype(BF16)
        lnm = jnp.stack([ln_m_g[l], ln_m_b[l]])
        x, v_first = _rwkv_call(z, g, m12, x, ada[l], v_first, mu.reshape(1, -1), wa2, rw_g2[l].astype(BF16), v2,
                                vecs, rw_wo[l].astype(BF16), w_out[l].astype(BF16), lnm, tm_rw, head, alpha)

        lnf = jnp.stack([ln_f_g[l], ln_f_b[l]])
        x = _mlp_call(x, ada[l], mlp_w1[l].astype(BF16), mlp_w2[l].astype(BF16), lnf, tm_ff, alpha)
    return x
```

```python
import functools
import math

import jax
import jax.numpy as jnp
from jax import lax
from jax.experimental import pallas as pl
from jax.experimental.pallas import tpu as pltpu

F32 = jnp.float32
BF16 = jnp.bfloat16

LN_EPS = 1e-5
RW_GN_EPS = 64e-5
POOL_WINDOWS = (2, 4, 8, 16)
KK_NORM_FLOOR = 1e-12
DECAY_SCALE = math.exp(-0.5)

LANES = 128
SUBLANES = 8
VMEM_LIMIT_BYTES = 56 * 1024 * 1024

CHUNK = 64
HEAD_GROUP = 4


def _dot(a, b):
    return jnp.dot(a.astype(BF16), b.astype(BF16), preferred_element_type=F32)


def _dot_nt(a, b):
    return lax.dot_general(a.astype(BF16), b.astype(BF16), (((1,), (1,)), ((), ())),
                           preferred_element_type=F32)


def _dot_tn(a, b):
    return lax.dot_general(a.astype(BF16), b.astype(BF16), (((0,), (0,)), ((), ())),
                           preferred_element_type=F32)


def _split3(x):
    hi = x.astype(BF16)
    r1 = x - hi.astype(F32)
    mid = r1.astype(BF16)
    lo = (r1 - mid.astype(F32)).astype(BF16)
    return hi, mid, lo


def _layernorm(x, g, b, eps):
    mu = jnp.mean(x, axis=-1, keepdims=True)
    d = x - mu
    var = jnp.mean(d * d, axis=-1, keepdims=True)
    return d * lax.rsqrt(var + eps) * g + b


def _sigmoid(x):
    return 0.5 * jnp.tanh(0.5 * x) + 0.5


def _ada_kernel(c_ref, w_ref, b_ref, o_ref):
    c = c_ref[...]
    cond = c * _sigmoid(c)
    c_hi, c_mid, c_lo = _split3(cond)
    w_hi, w_mid, w_lo = _split3(w_ref[...])
    dd = lambda a, b: jnp.dot(a, b, preferred_element_type=F32)
    acc = dd(c_hi, w_hi) + (dd(c_hi, w_mid) + dd(c_mid, w_hi)) + (dd(c_hi, w_lo) + dd(c_mid, w_mid) + dd(c_lo, w_hi))
    o_ref[...] = acc + b_ref[...]


def _ada_call(c, ada_w, ada_b):
    depth, d, nd = ada_w.shape
    n_ada = nd // d
    bsz = c.shape[0]
    out = pl.pallas_call(
        _ada_kernel,
        grid=(depth, n_ada),
        in_specs=[
            pl.BlockSpec((bsz, d), lambda l, j: (0, 0)),
            pl.BlockSpec((None, d, d), lambda l, j: (l, 0, j)),
            pl.BlockSpec((None, None, 1, d), lambda l, j: (l, j, 0, 0)),
        ],
        out_specs=pl.BlockSpec((None, None, bsz, d), lambda l, j: (l, j, 0, 0)),
        out_shape=jax.ShapeDtypeStruct((depth, n_ada, bsz, d), F32),
        name="ada",
    )(c, ada_w, ada_b.reshape(depth, n_ada, 1, d))
    return jnp.transpose(out, (0, 2, 1, 3))


def _inproj_kernel(x_ref, ada_ref, wz_ref, wu_ref, wp_ref, wg_ref, z_ref, u_ref, p_ref, g_ref):
    sh = ada_ref[0:1, :]
    sc = ada_ref[1:2, :]
    h = (x_ref[...] * (1.0 + sc) + sh).astype(BF16)
    for w_ref, o_ref in ((wz_ref, z_ref), (wu_ref, u_ref), (wp_ref, p_ref), (wg_ref, g_ref)):
        o_ref[...] = jnp.dot(h, w_ref[...], preferred_element_type=F32).astype(o_ref.dtype)


def _inproj_call(x, ada, wz, wu, wp, wg, tm):
    bsz, s, d = x.shape
    widths = [w.shape[1] for w in (wz, wu, wp, wg)]
    const = lambda b, i: (0, 0)
    return pl.pallas_call(
        _inproj_kernel,
        grid=(bsz, s // tm),
        in_specs=[
            pl.BlockSpec((None, tm, d), lambda b, i: (b, i, 0)),
            pl.BlockSpec((None,) + ada.shape[1:], lambda b, i: (b, 0, 0)),
        ] + [pl.BlockSpec((d, w), const, pipeline_mode=pl.Buffered(1)) for w in widths],
        out_specs=[pl.BlockSpec((None, tm, w), lambda b, i: (b, i, 0)) for w in widths],
        out_shape=[jax.ShapeDtypeStruct((bsz, s, w), BF16) for w in widths],
        compiler_params=pltpu.CompilerParams(
            dimension_semantics=("parallel", "arbitrary"), vmem_limit_bytes=VMEM_LIMIT_BYTES),
        name="inproj",
    )(x, ada, wz, wu, wp, wg)


def _cvpl_kernel(u_ref, p_ref, g1_ref, g2_ref, cvw_ref, cvb_ref, lng_ref, lnb_ref, cvwo_ref,
                 plw_ref, pls_ref, plwo_ref, o_ref, hbuf, shbuf, pbuf, *, tm, cv, conv_width, pg):
    i = pl.program_id(1)
    halo_c = hbuf.shape[0] - tm
    halo_p = pbuf.shape[0] - tm

    @pl.when(i == 0)
    def _():
        hbuf[0:halo_c, :] = jnp.zeros((halo_c, cv), F32)
        pbuf[0:halo_p, :] = jnp.zeros((halo_p, pbuf.shape[1]), F32)

    u = u_ref[...].astype(F32)
    hbuf[halo_c:halo_c + tm, :] = u[:, :cv] * _sigmoid(u[:, cv:])
    base = halo_c - (conv_width - 1)
    span = shbuf.shape[1]
    for r in range(1, SUBLANES):
        shbuf[r - 1] = hbuf[r:r + span, :]
    acc = None
    for j in range(conv_width):
        q, r = divmod(base + j, SUBLANES)
        src = hbuf[q * SUBLANES:q * SUBLANES + tm, :] if r == 0 else shbuf[r - 1, q * SUBLANES:q * SUBLANES + tm, :]
        term = src * cvw_ref[j:j + 1, :]
        acc = term if acc is None else acc + term
    acc = acc + cvb_ref[...]
    hn = _layernorm(acc, lng_ref[...], lnb_ref[...], LN_EPS)
    hn = hn * _sigmoid(hn)
    y_cv = _dot(hn, cvwo_ref[...])
    hbuf[0:halo_c, :] = hbuf[tm:tm + halo_c, :]

    pf = p_ref[...].astype(F32)
    pbuf[halo_p:halo_p + tm, :] = pf
    t1 = (lax.broadcasted_iota(jnp.int32, (tm, 1), 0) + (i * tm + 1)).astype(F32)
    mixed = []
    for gi, win in enumerate(POOL_WINDOWS):
        sl = slice(gi * pg, (gi + 1) * pg)
        ssum = pbuf[halo_p:halo_p + tm, sl]
        for d in range(1, win):
            ssum = ssum + pbuf[halo_p - d:halo_p - d + tm, sl]
        pooled = ssum / jnp.minimum(t1, float(win)) - pf[:, sl]
        mixed.append(_dot(pooled, plw_ref[gi]))
    mixed = jnp.concatenate(mixed, axis=1) * pls_ref[...]
    y_pl = _dot(mixed, plwo_ref[...])
    pbuf[0:halo_p, :] = pbuf[tm:tm + halo_p, :]

    o_ref[...] = (_sigmoid(g1_ref[...].astype(F32)) * y_cv
                  + _sigmoid(g2_ref[...].astype(F32)) * y_pl)


def _cvpl_call(u, p, g, cvw, cvb, lng, lnb, cvwo, plw, pls, plwo, tm):
    bsz, s, _ = u.shape
    conv_width, cv = cvw.shape
    pw = p.shape[-1]
    d = cvwo.shape[1]
    pg = plw.shape[1]
    halo_c = -(-(conv_width - 1) // SUBLANES) * SUBLANES
    halo_p = -(-(max(POOL_WINDOWS) - 1) // SUBLANES) * SUBLANES
    c2 = lambda b, i: (0, 0)
    kern = functools.partial(_cvpl_kernel, tm=tm, cv=cv, conv_width=conv_width, pg=pg)
    return pl.pallas_call(
        kern,
        grid=(bsz, s // tm),
        in_specs=[
            pl.BlockSpec((None, tm, 2 * cv), lambda b, i: (b, i, 0)),
            pl.BlockSpec((None, tm, pw), lambda b, i: (b, i, 0)),
            pl.BlockSpec((None, tm, d), lambda b, i: (b, i, 1)),
            pl.BlockSpec((None, tm, d), lambda b, i: (b, i, 2)),
            pl.BlockSpec(cvw.shape, c2),
            pl.BlockSpec((1, cv), c2),
            pl.BlockSpec((1, cv), c2),
            pl.BlockSpec((1, cv), c2),
            pl.BlockSpec(cvwo.shape, c2),
            pl.BlockSpec(plw.shape, lambda b, i: (0, 0, 0)),
            pl.BlockSpec((1, pw), c2),
            pl.BlockSpec(plwo.shape, c2),
        ],
        out_specs=pl.BlockSpec((None, tm, d), lambda b, i: (b, i, 0)),
        out_shape=jax.ShapeDtypeStruct((bsz, s, d), F32),
        scratch_shapes=[pltpu.VMEM((tm + halo_c, cv), F32),
                        pltpu.VMEM((SUBLANES - 1, tm + halo_c - SUBLANES, cv), F32),
                        pltpu.VMEM((tm + halo_p, pw), F32)],
        compiler_params=pltpu.CompilerParams(
            dimension_semantics=("parallel", "arbitrary"), vmem_limit_bytes=VMEM_LIMIT_BYTES),
        name="cvpl",
    )(u, p, g, g, cvw, cvb.reshape(1, cv), lng.reshape(1, cv), lnb.reshape(1, cv), cvwo,
      plw, pls.reshape(1, pw), plwo)


def _head_sum(x, head):
    per_slab = LANES // head
    lane = lax.broadcasted_iota(jnp.int32, (1, LANES), 1)
    out = []
    for s0 in range(0, x.shape[1], LANES):
        slab = x[:, s0:s0 + LANES]
        res = jnp.zeros_like(slab)
        for q in range(per_slab):
            m = (lane >= q * head) & (lane < (q + 1) * head)
            tot = jnp.sum(jnp.where(m, slab, 0.0), axis=-1, keepdims=True)
            res = jnp.where(m, tot, res)
        out.append(res)
    return jnp.concatenate(out, axis=1)


def _rwkv_kernel(*refs, tm, rw, head, has_vres, alpha):
    if has_vres:
        (z_ref, g0_ref, m12_ref, x_ref, ada_ref, vf_ref, mu_ref, wa2_ref, g2_ref, v2_ref, vec_ref, wo_ref,
         wout_ref, lnv_ref, o_ref,
         zbuf, r_s, lw_s, k_s, v_s, a_s, b_s, g_s, bonus_s, y_s, h_s) = refs
        vfo_ref = None
    else:
        (z_ref, g0_ref, m12_ref, x_ref, ada_ref, mu_ref, wa2_ref, g2_ref, vec_ref, wo_ref,
         wout_ref, lnv_ref, o_ref, vfo_ref,
         zbuf, r_s, lw_s, k_s, v_s, a_s, b_s, g_s, bonus_s, y_s, h_s) = refs
        vf_ref = v2_ref = None
    i = pl.program_id(1)
    gw = HEAD_GROUP * head
    n_grp = rw // gw

    @pl.when(i == 0)
    def _():
        zbuf[0:SUBLANES, :] = jnp.zeros((SUBLANES, zbuf.shape[1]), F32)
        h_s[...] = jnp.zeros(h_s.shape, F32)

    zr = z_ref[...].astype(F32)
    zbuf[SUBLANES:SUBLANES + tm, :] = zr
    prev = zbuf[SUBLANES - 1:SUBLANES - 1 + tm, :]
    z = zr + (prev - zr) * mu_ref[...]
    zbuf[0:SUBLANES, :] = zbuf[tm:tm + SUBLANES, :]

    r = z[:, 0:rw]
    k = z[:, rw:2 * rw]
    v = z[:, 2 * rw:3 * rw]
    lwa = z[:, 3 * rw:3 * rw + LANES]
    lg = z[:, 3 * rw + LANES:3 * rw + 2 * LANES]
    w0, a0, k_k, k_a, r_k, gn_g, gn_b, v0 = (vec_ref[j:j + 1, :] for j in range(8))

    lane = lax.broadcasted_iota(jnp.int32, (1, LANES), 1)
    lora_in = jnp.where(lane < LANES // 2, jnp.tanh(lwa), lwa)
    wl = _dot(lora_in, wa2_ref[...])
    lwd = -DECAY_SCALE * _sigmoid(w0 + wl[:, 0:rw])
    a = _sigmoid(a0 + wl[:, rw:2 * rw])
    g_s[...] = _dot(_sigmoid(lg), g2_ref[...])
    if has_vres:
        lv = z[:, 3 * rw + 2 * LANES:3 * rw + 3 * LANES]
        v = v + (vf_ref[...] - v) * _sigmoid(v0 + _dot(lv, v2_ref[...]))
    else:
        vfo_ref[...] = v
    kk = k * k_k
    kkn = kk * jnp.minimum(lax.rsqrt(_head_sum(kk * kk, head)), 1.0 / KK_NORM_FLOOR)
    kp = k * (1.0 + (a - 1.0) * k_a)
    bonus_s[...] = _head_sum(r * kp * r_k, head) * v
    r_s[...] = r
    lw_s[...] = lwd
    k_s[...] = kp
    v_s[...] = v
    a_s[...] = -kkn
    b_s[...] = kkn * a

    L = CHUNK
    ri = lax.broadcasted_iota(jnp.int32, (L, L), 0)
    ci = lax.broadcasted_iota(jnp.int32, (L, L), 1)
    tri = jnp.where(ri >= ci, 1.0, 0.0).astype(BF16)
    crow = lax.broadcasted_iota(jnp.int32, (L, gw), 0)
    ccol = lax.broadcasted_iota(jnp.int32, (L, gw), 1) % L
    cmask_s = crow > ccol
    cmask_i = crow >= ccol
    eye_c = crow == ccol
    eye_f = jnp.where(eye_c, 1.0, 0.0)
    glane = lax.broadcasted_iota(jnp.int32, (1, gw), 1)
    head_masks = [(glane >= h * head) & (glane < (h + 1) * head) for h in range(HEAD_GROUP)]
    mid = L // 2 - 1
    n_dbl = int(math.log2(L))

    def stack(x):
        xb = x.astype(BF16)
        zero = jnp.zeros_like(xb)
        return jnp.concatenate([jnp.where(m, xb, zero) for m in head_masks], axis=0)

    n_chunk = tm // L
    decay = []
    for c in range(n_chunk):
        rows = slice(c * L, (c + 1) * L)
        lw_c = lw_s[rows, :]
        hi, md, _ = _split3(lw_c)
        dd = lambda a_, b_: jnp.dot(a_, b_, preferred_element_type=F32)
        cl = dd(tri, hi) + dd(tri, md)
        c_mid = cl[mid:mid + 1, :]
        c_end = cl[L - 1:L, :]
        decay.append(dict(w_inc=jnp.exp(cl - c_mid), w_exc=jnp.exp(cl - lw_c - c_mid),
                          w_inv=jnp.exp(c_mid - cl), w_end=jnp.exp(c_end - cl),
                          e0=jnp.exp(c_mid), wl_end=jnp.exp(c_end)))

    inst = [dict(c=c, gi=gi, rows=slice(c * L, (c + 1) * L), ls=slice(gi * gw, (gi + 1) * gw))
            for c in range(n_chunk) for gi in range(n_grp)]

    for st in inst:
        rows, ls, dc = st["rows"], st["ls"], decay[st["c"]]
        st["e0"] = dc["e0"][:, ls]
        st["wl_end"] = dc["wl_end"][:, ls]
        a_t = a_s[rows, ls] * dc["w_exc"][:, ls]
        st["r_t"] = r_s[rows, ls] * dc["w_inc"][:, ls]
        b_t = b_s[rows, ls] * dc["w_inv"][:, ls]
        k_t = k_s[rows, ls] * dc["w_inv"][:, ls]
        b_h = b_s[rows, ls] * dc["w_end"][:, ls]
        k_h = k_s[rows, ls] * dc["w_end"][:, ls]
        st["sv"] = stack(v_s[rows, ls])
        st["sa"] = stack(a_t)
        st["sbk_h"] = jnp.concatenate([stack(b_h), stack(k_h)], axis=0)
        m4 = _dot_nt(jnp.concatenate([a_t, st["r_t"]], axis=0),
                     jnp.concatenate([stack(b_t), stack(k_t)], axis=0))
        st["c_ab"] = jnp.where(cmask_s, m4[0:L, 0:gw], 0.0)
        st["c_ak"] = jnp.where(cmask_s, m4[0:L, gw:2 * gw], 0.0)
        st["c_rb"] = jnp.where(cmask_i, m4[L:2 * L, 0:gw], 0.0)
        st["c_rk"] = jnp.where(cmask_i, m4[L:2 * L, gw:2 * gw], 0.0)

    for st in inst:
        st["s_pow"] = _dot(st["c_ab"], stack(st["c_ab"]))
        st["t_c"] = eye_f + st["c_ab"]
        st["v2"] = _dot(st["c_ak"], st["sv"])
        st["ct"] = _dot_nt(eye_f, st["sbk_h"])

    for it in range(1, n_dbl):
        for st in inst:
            bd = stack(st["s_pow"])
            if it < n_dbl - 1:
                prod = _dot(jnp.concatenate([st["t_c"], st["s_pow"]], axis=0), bd)
                st["t_c"] = st["t_c"] + prod[0:L]
                st["s_pow"] = prod[L:2 * L]
            else:
                st["t_c"] = st["t_c"] + _dot(st["t_c"], bd)

    for st in inst:
        x2 = _dot(st["t_c"], jnp.concatenate([st["sa"], stack(st["v2"])], axis=1))
        st["s_x2"] = jnp.concatenate([stack(x2[:, 0:gw]), stack(x2[:, gw:2 * gw])], axis=1)

    for st in inst:
        w1 = _dot(st["c_rb"], st["s_x2"])
        st["rhat"] = (st["r_t"] + w1[:, 0:gw]) * st["e0"]
        st["y0"] = w1[:, gw:2 * gw] + _dot(st["c_rk"], st["sv"])
        pq = _dot(st["ct"][:, 0:gw], st["s_x2"])
        st["p_c"] = jnp.where(eye_c, st["wl_end"], 0.0) + pq[:, 0:gw] * st["e0"]
        st["q_c"] = pq[:, gw:2 * gw] + _dot(st["ct"][:, gw:2 * gw], st["sv"])

    for st in inst:
        gi = st["gi"]
        sh = stack(h_s[gi])
        y_s[st["rows"], st["ls"]] = _dot(st["rhat"], sh) + st["y0"]
        h_s[gi] = _dot(st["p_c"], sh) + st["q_c"]

    y = y_s[...]
    mu = _head_sum(y, head) * (1.0 / head)
    dlt = y - mu
    var = _head_sum(dlt * dlt, head) * (1.0 / head)
    yn = dlt * lax.rsqrt(var + RW_GN_EPS) * gn_g + gn_b
    y_rw = _dot((yn + bonus_s[...]) * g_s[...], wo_ref[...])
    merged = _sigmoid(g0_ref[...].astype(F32)) * y_rw + m12_ref[...]
    o = _dot(merged, wout_ref[...])
    gt = ada_ref[2:3, :]
    o_ref[...] = _layernorm(alpha * x_ref[...] + gt * o, lnv_ref[0:1, :], lnv_ref[1:2, :], LN_EPS)


def _rwkv_call(z, g, m12, x, ada, vfirst, mu, wa2, g2, v2, vecs, wo, wout, lnv, tm, head, alpha):
    bsz, s, d = x.shape
    zw = z.shape[-1]
    rw = wo.shape[0]
    has_vres = vfirst is not None
    c2 = lambda b, i: (0, 0)
    tile = lambda w: pl.BlockSpec((None, tm, w), lambda b, i: (b, i, 0))
    in_specs = [tile(zw), tile(d), tile(d), tile(d),
                pl.BlockSpec((None,) + ada.shape[1:], lambda b, i: (b, 0, 0))]
    args = [z, g, m12, x, ada]
    if has_vres:
        in_specs.append(tile(rw))
        args.append(vfirst)
    in_specs += [pl.BlockSpec(mu.shape, c2), pl.BlockSpec(wa2.shape, c2), pl.BlockSpec(g2.shape, c2)]
    args += [mu, wa2, g2]
    if has_vres:
        in_specs.append(pl.BlockSpec(v2.shape, c2))
        args.append(v2)
    in_specs += [pl.BlockSpec(vecs.shape, c2), pl.BlockSpec(wo.shape, c2), pl.BlockSpec(wout.shape, c2),
                 pl.BlockSpec(lnv.shape, c2)]
    args += [vecs, wo, wout, lnv]
    out_specs = [tile(d)]
    out_shape = [jax.ShapeDtypeStruct((bsz, s, d), F32)]
    if not has_vres:
        out_specs.append(tile(rw))
        out_shape.append(jax.ShapeDtypeStruct((bsz, s, rw), F32))
    gw = HEAD_GROUP * head
    scratch = [pltpu.VMEM((tm + SUBLANES, zw), F32)]
    scratch += [pltpu.VMEM((tm, rw), F32) for _ in range(9)]
    scratch += [pltpu.VMEM((rw // gw, CHUNK, gw), F32)]
    kern = functools.partial(_rwkv_kernel, tm=tm, rw=rw, head=head, has_vres=has_vres, alpha=alpha)
    res = pl.pallas_call(
        kern,
        grid=(bsz, s // tm),
        in_specs=in_specs,
        out_specs=out_specs,
        out_shape=out_shape,
        scratch_shapes=scratch,
        compiler_params=pltpu.CompilerParams(
            dimension_semantics=("parallel", "arbitrary"), vmem_limit_bytes=VMEM_LIMIT_BYTES),
        name="rwkv_vres" if has_vres else "rwkv",
    )(*args)
    return (res[0], vfirst) if has_vres else (res[0], res[1])


def _mlp_kernel(x_ref, ada_ref, w1_ref, w2_ref, lnv_ref, o_ref, *, alpha, n_split):
    x = x_ref[...]
    sh = ada_ref[3:4, :]
    sc = ada_ref[4:5, :]
    gt = ada_ref[5:6, :]
    h = (x * (1.0 + sc) + sh).astype(BF16)
    dff = w1_ref.shape[1]
    step = dff // n_split
    acc = None
    for j in range(n_split):
        t = jnp.dot(h, w1_ref[:, j * step:(j + 1) * step], preferred_element_type=F32)
        t = jnp.square(jnp.maximum(t, 0.0)).astype(BF16)
        part = jnp.dot(t, w2_ref[j * step:(j + 1) * step, :], preferred_element_type=F32)
        acc = part if acc is None else acc + part
    o_ref[...] = _layernorm(alpha * x + gt * acc, lnv_ref[0:1, :], lnv_ref[1:2, :], LN_EPS)


def _mlp_call(x, ada, w1, w2, lnv, tm, alpha):
    bsz, s, d = x.shape
    dff = w1.shape[1]
    c2 = lambda b, i: (0, 0)
    kern = functools.partial(_mlp_kernel, alpha=alpha, n_split=dff // d)
    return pl.pallas_call(
        kern,
        grid=(bsz, s // tm),
        in_specs=[
            pl.BlockSpec((None, tm, d), lambda b, i: (b, i, 0)),
            pl.BlockSpec((None,) + ada.shape[1:], lambda b, i: (b, 0, 0)),
            pl.BlockSpec(w1.shape, c2, pipeline_mode=pl.Buffered(1)),
            pl.BlockSpec(w2.shape, c2, pipeline_mode=pl.Buffered(1)),
            pl.BlockSpec(lnv.shape, c2),
        ],
        out_specs=pl.BlockSpec((None, tm, d), lambda b, i: (b, i, 0)),
        out_shape=jax.ShapeDtypeStruct((bsz, s, d), F32),
        compiler_params=pltpu.CompilerParams(
            dimension_semantics=("parallel", "arbitrary"), vmem_limit_bytes=VMEM_LIMIT_BYTES),
        name="mlp",
    )(x, ada, w1, w2, lnv)


def _pick_tile(s, pref):
    tm = min(pref, s)
    assert s % tm == 0 and tm % CHUNK == 0, (s, tm)
    return tm


def kernel(x, c, ada_w, ada_b, w_in, w_in_vres, shift_mu, shift_mu_vres, rw_w0, rw_w2, rw_a0, rw_a2, rw_g2, rw_v0, rw_v2, rw_kk, rw_ka, rw_rk, rw_gn_g, rw_gn_b, rw_wo, cv_w, cv_b, cv_ln_g, cv_ln_b, cv_wo, pl_w, pl_scale, pl_wo, w_out, ln_m_g, ln_m_b, mlp_w1, mlp_w2, ln_f_g, ln_f_b):
    bsz, s, d = x.shape
    depth = ada_w.shape[0]
    heads, head = rw_rk.shape[1], rw_rk.shape[2]
    rw = heads * head
    lw_w, la_w, lg_w = rw_w2.shape[1], rw_a2.shape[1], rw_g2.shape[1]
    lv_w = rw_v2.shape[1]
    cv = cv_w.shape[2]
    pw = pl_scale.shape[1]
    assert lw_w + la_w == LANES and lg_w == LANES and lv_w <= LANES and LANES % head == 0
    assert rw % (HEAD_GROUP * head) == 0 and head == CHUNK
    alpha = float((2 * depth) ** 0.25)
    rw_shift = 3 * rw + lw_w + la_w + lg_w
    cv_off, pl_off = rw_shift, rw_shift + 2 * cv
    gt_off = pl_off + pw

    tm_in = _pick_tile(s, 512)
    tm_cv = _pick_tile(s, 512)
    tm_rw = _pick_tile(s, 256)
    tm_ff = _pick_tile(s, 512)

    ada = _ada_call(c, ada_w, ada_b)

    v_first = None
    for l in range(depth):
        has_vres = l > 0
        w_l = w_in[l]
        wz = w_l[:, :rw_shift]
        mu = shift_mu[l]
        if has_vres:
            pad = LANES - lv_w
            wz = jnp.concatenate([wz, w_in_vres[l - 1], jnp.zeros((d, pad), F32)], axis=1)
            mu = jnp.concatenate([mu, shift_mu_vres[l - 1], jnp.zeros((pad,), F32)])
        wz, wu, wp, wg = (t.astype(BF16) for t in (wz, w_l[:, cv_off:pl_off], w_l[:, pl_off:gt_off], w_l[:, gt_off:]))
        z, u, p, g = _inproj_call(x, ada[l], wz, wu, wp, wg, tm_in)

        m12 = _cvpl_call(u, p, g, cv_w[l], cv_b[l], cv_ln_g[l], cv_ln_b[l], cv_wo[l].astype(BF16),
                         pl_w[l].astype(BF16), pl_scale[l], pl_wo[l].astype(BF16), tm_cv)

        wa2 = jnp.zeros((LANES, 2 * rw), F32)
        wa2 = wa2.at[:lw_w, :rw].set(rw_w2[l]).at[lw_w:, rw:].set(rw_a2[l]).astype(BF16)
        v0 = rw_v0[l - 1] if has_vres else jnp.zeros((rw,), F32)
        vecs = jnp.stack([rw_w0[l], rw_a0[l], rw_kk[l], rw_ka[l], rw_rk[l].reshape(rw),
                          rw_gn_g[l], rw_gn_b[l], v0])
        v2 = None
        if has_vres:
            v2 = jnp.zeros((LANES, rw), F32).at[:lv_w].set(rw_v2[l - 1]).astype(BF16)
        lnm = jnp.stack([ln_m_g[l], ln_m_b[l]])
        x, v_first = _rwkv_call(z, g, m12, x, ada[l], v_first, mu.reshape(1, -1), wa2, rw_g2[l].astype(BF16), v2,
                                vecs, rw_wo[l].astype(BF16), w_out[l].astype(BF16), lnm, tm_rw, head, alpha)

        lnf = jnp.stack([ln_f_g[l], ln_f_b[l]])
        x = _mlp_call(x, ada[l], mlp_w1[l].astype(BF16), mlp_w2[l].astype(BF16), lnf, tm_ff, alpha)
    return x
```

```python
import functools
import math

import jax
import jax.numpy as jnp
from jax import lax
from jax.experimental import pallas as pl
from jax.experimental.pallas import tpu as pltpu

F32 = jnp.float32
BF16 = jnp.bfloat16

LN_EPS = 1e-5
RW_GN_EPS = 64e-5
POOL_WINDOWS = (2, 4, 8, 16)
KK_NORM_FLOOR = 1e-12
DECAY_SCALE = math.exp(-0.5)

LANES = 128
SUBLANES = 8
VMEM_LIMIT_BYTES = 56 * 1024 * 1024

CHUNK = 64
HEAD_GROUP = 4


def _dot(a, b):
    return jnp.dot(a.astype(BF16), b.astype(BF16), preferred_element_type=F32)


def _dot_nt(a, b):
    return lax.dot_general(a.astype(BF16), b.astype(BF16), (((1,), (1,)), ((), ())),
                           preferred_element_type=F32)


def _dot_tn(a, b):
    return lax.dot_general(a.astype(BF16), b.astype(BF16), (((0,), (0,)), ((), ())),
                           preferred_element_type=F32)


def _split3(x):
    hi = x.astype(BF16)
    r1 = x - hi.astype(F32)
    mid = r1.astype(BF16)
    lo = (r1 - mid.astype(F32)).astype(BF16)
    return hi, mid, lo


def _layernorm(x, g, b, eps):
    mu = jnp.mean(x, axis=-1, keepdims=True)
    d = x - mu
    var = jnp.mean(d * d, axis=-1, keepdims=True)
    return d * lax.rsqrt(var + eps) * g + b


def _sigmoid(x):
    return 0.5 * jnp.tanh(0.5 * x) + 0.5


def _ada_kernel(c_ref, w_ref, b_ref, o_ref):
    c = c_ref[...]
    cond = c * _sigmoid(c)
    c_hi, c_mid, c_lo = _split3(cond)
    w_hi, w_mid, w_lo = _split3(w_ref[...])
    dd = lambda a, b: jnp.dot(a, b, preferred_element_type=F32)
    acc = dd(c_hi, w_hi) + (dd(c_hi, w_mid) + dd(c_mid, w_hi)) + (dd(c_hi, w_lo) + dd(c_mid, w_mid) + dd(c_lo, w_hi))
    o_ref[...] = acc + b_ref[...]


def _ada_call(c, ada_w, ada_b):
    depth, d, nd = ada_w.shape
    n_ada = nd // d
    bsz = c.shape[0]
    out = pl.pallas_call(
        _ada_kernel,
        grid=(depth, n_ada),
        in_specs=[
            pl.BlockSpec((bsz, d), lambda l, j: (0, 0)),
            pl.BlockSpec((None, d, d), lambda l, j: (l, 0, j)),
            pl.BlockSpec((None, None, 1, d), lambda l, j: (l, j, 0, 0)),
        ],
        out_specs=pl.BlockSpec((None, None, bsz, d), lambda l, j: (l, j, 0, 0)),
        out_shape=jax.ShapeDtypeStruct((depth, n_ada, bsz, d), F32),
        name="ada",
    )(c, ada_w, ada_b.reshape(depth, n_ada, 1, d))
    return jnp.transpose(out, (0, 2, 1, 3))


CONV_ROW_BLOCK = 128
PROJ_COL_CHUNK = 512


def _front_kernel(x_ref, ada_ref, wz_ref, wu_ref, wp_ref, wg_ref, cvw_ref, cvb_ref, lng_ref, lnb_ref, cvwo_ref,
                  plw_ref, pls_ref, plwo_ref, z_ref, g0_ref, m12_ref, hbuf, shbuf, pbuf, acc_s, gate_s,
                  *, tm, cv, conv_width, pg):
    i = pl.program_id(1)
    d = g0_ref.shape[1]
    halo_c = hbuf.shape[0] - tm
    halo_p = pbuf.shape[0] - tm

    @pl.when(i == 0)
    def _():
        hbuf[0:halo_c, :] = jnp.zeros((halo_c, cv), F32)
        pbuf[0:halo_p, :] = jnp.zeros((halo_p, pbuf.shape[1]), F32)

    sh = ada_ref[0:1, :]
    sc = ada_ref[1:2, :]
    h = (x_ref[...] * (1.0 + sc) + sh).astype(BF16)

    u = jnp.dot(h, wu_ref[...], preferred_element_type=F32)
    hbuf[halo_c:halo_c + tm, :] = u[:, :cv] * _sigmoid(u[:, cv:])
    pf = jnp.dot(h, wp_ref[...], preferred_element_type=F32)
    pbuf[halo_p:halo_p + tm, :] = pf

    def z_job(c0, c1):
        def run():
            z_ref[:, c0:c1] = jnp.dot(h, wz_ref[:, c0:c1], preferred_element_type=F32).astype(z_ref.dtype)
        return run

    def g_job(c0, c1):
        def run():
            gv = jnp.dot(h, wg_ref[:, c0:c1], preferred_element_type=F32)
            if c0 < d:
                g0_ref[:, c0:c1] = gv.astype(g0_ref.dtype)
            else:
                gate_s[:, c0 - d:c1 - d] = _sigmoid(gv)
        return run

    zw = z_ref.shape[1]
    jobs = [z_job(c0, min(c0 + PROJ_COL_CHUNK, zw)) for c0 in range(0, zw, PROJ_COL_CHUNK)]
    jobs += [g_job(c0, c0 + PROJ_COL_CHUNK) for c0 in range(0, wg_ref.shape[1], PROJ_COL_CHUNK)]

    base = halo_c - (conv_width - 1)
    span = shbuf.shape[2]
    n_blocks = (cv // LANES) * (tm // CONV_ROW_BLOCK)
    n_jobs = len(jobs)
    n_done = 0
    for sl in range(cv // LANES):
        cs = slice(sl * LANES, (sl + 1) * LANES)
        sb = shbuf.at[sl % 2]
        for r in range(1, SUBLANES):
            sb[r - 1] = hbuf[r:r + span, cs]
        for rb in range(tm // CONV_ROW_BLOCK):
            r0 = rb * CONV_ROW_BLOCK
            acc = None
            for j in range(conv_width):
                q, r = divmod(base + j, SUBLANES)
                lo = q * SUBLANES + r0
                src = hbuf[lo:lo + CONV_ROW_BLOCK, cs] if r == 0 else sb[r - 1, lo:lo + CONV_ROW_BLOCK, :]
                term = src * cvw_ref[j:j + 1, cs]
                acc = term if acc is None else acc + term
            acc_s[r0:r0 + CONV_ROW_BLOCK, cs] = acc + cvb_ref[:, cs]
            n_done += 1
            while jobs and (n_jobs - len(jobs)) * n_blocks < n_done * n_jobs:
                jobs.pop(0)()
    assert not jobs
    hbuf[0:halo_c, :] = hbuf[tm:tm + halo_c, :]

    hn = _layernorm(acc_s[...], lng_ref[...], lnb_ref[...], LN_EPS)
    hn = hn * _sigmoid(hn)
    y_cv = _dot(hn, cvwo_ref[...])

    t1 = (lax.broadcasted_iota(jnp.int32, (tm, 1), 0) + (i * tm + 1)).astype(F32)
    mixed = []
    for gi, win in enumerate(POOL_WINDOWS):
        sl = slice(gi * pg, (gi + 1) * pg)
        ssum = pbuf[halo_p:halo_p + tm, sl]
        for dlt in range(1, win):
            ssum = ssum + pbuf[halo_p - dlt:halo_p - dlt + tm, sl]
        pooled = ssum / jnp.minimum(t1, float(win)) - pbuf[halo_p:halo_p + tm, sl]
        mixed.append(_dot(pooled, plw_ref[gi]))
    mixed = jnp.concatenate(mixed, axis=1) * pls_ref[...]
    y_pl = _dot(mixed, plwo_ref[...])
    pbuf[0:halo_p, :] = pbuf[tm:tm + halo_p, :]

    m12_ref[...] = gate_s[:, 0:d] * y_cv + gate_s[:, d:2 * d] * y_pl


def _front_call(x, ada, wz, wu, wp, wg, cvw, cvb, lng, lnb, cvwo, plw, pls, plwo, tm):
    bsz, s, d = x.shape
    conv_width, cv = cvw.shape
    pw = wp.shape[1]
    pg = plw.shape[1]
    zw = wz.shape[1]
    halo_c = -(-(conv_width - 1) // SUBLANES) * SUBLANES
    halo_p = -(-(max(POOL_WINDOWS) - 1) // SUBLANES) * SUBLANES
    c2 = lambda b, i: (0, 0)
    tile = lambda w: pl.BlockSpec((None, tm, w), lambda b, i: (b, i, 0))
    resident = lambda a: pl.BlockSpec(a.shape, c2, pipeline_mode=pl.Buffered(1))
    kern = functools.partial(_front_kernel, tm=tm, cv=cv, conv_width=conv_width, pg=pg)
    return pl.pallas_call(
        kern,
        grid=(bsz, s // tm),
        in_specs=[
            tile(d),
            pl.BlockSpec((None,) + ada.shape[1:], lambda b, i: (b, 0, 0)),
            resident(wz), resident(wu), resident(wp), resident(wg),
            pl.BlockSpec(cvw.shape, c2),
            pl.BlockSpec((1, cv), c2),
            pl.BlockSpec((1, cv), c2),
            pl.BlockSpec((1, cv), c2),
            resident(cvwo),
            pl.BlockSpec(plw.shape, lambda b, i: (0, 0, 0)),
            pl.BlockSpec((1, pw), c2),
            resident(plwo),
        ],
        out_specs=[tile(zw), tile(d), tile(d)],
        out_shape=[jax.ShapeDtypeStruct((bsz, s, zw), BF16), jax.ShapeDtypeStruct((bsz, s, d), BF16),
                   jax.ShapeDtypeStruct((bsz, s, d), F32)],
        scratch_shapes=[pltpu.VMEM((tm + halo_c, cv), F32),
                        pltpu.VMEM((2, SUBLANES - 1, tm + halo_c - SUBLANES, LANES), F32),
                        pltpu.VMEM((tm + halo_p, pw), F32),
                        pltpu.VMEM((tm, cv), F32),
                        pltpu.VMEM((tm, 2 * d), F32)],
        compiler_params=pltpu.CompilerParams(
            dimension_semantics=("parallel", "arbitrary"), vmem_limit_bytes=VMEM_LIMIT_BYTES),
        name="front",
    )(x, ada, wz, wu, wp, wg, cvw, cvb.reshape(1, cv), lng.reshape(1, cv), lnb.reshape(1, cv), cvwo,
      plw, pls.reshape(1, pw), plwo)


def _head_sum(x, head):
    per_slab = LANES // head
    lane = lax.broadcasted_iota(jnp.int32, (1, LANES), 1)
    out = []
    for s0 in range(0, x.shape[1], LANES):
        slab = x[:, s0:s0 + LANES]
        res = jnp.zeros_like(slab)
        for q in range(per_slab):
            m = (lane >= q * head) & (lane < (q + 1) * head)
            tot = jnp.sum(jnp.where(m, slab, 0.0), axis=-1, keepdims=True)
            res = jnp.where(m, tot, res)
        out.append(res)
    return jnp.concatenate(out, axis=1)


def _rwkv_kernel(*refs, tm, rw, head, has_vres, alpha):
    if has_vres:
        (z_ref, g0_ref, m12_ref, x_ref, ada_ref, vf_ref, mu_ref, wa2_ref, g2_ref, v2_ref, vec_ref, wo_ref,
         wout_ref, lnv_ref, o_ref,
         zbuf, r_s, lw_s, k_s, v_s, a_s, b_s, g_s, bonus_s, y_s, h_s) = refs
        vfo_ref = None
    else:
        (z_ref, g0_ref, m12_ref, x_ref, ada_ref, mu_ref, wa2_ref, g2_ref, vec_ref, wo_ref,
         wout_ref, lnv_ref, o_ref, vfo_ref,
         zbuf, r_s, lw_s, k_s, v_s, a_s, b_s, g_s, bonus_s, y_s, h_s) = refs
        vf_ref = v2_ref = None
    i = pl.program_id(1)
    gw = HEAD_GROUP * head
    n_grp = rw // gw

    @pl.when(i == 0)
    def _():
        zbuf[0:SUBLANES, :] = jnp.zeros((SUBLANES, zbuf.shape[1]), F32)
        h_s[...] = jnp.zeros(h_s.shape, F32)

    zr = z_ref[...].astype(F32)
    zbuf[SUBLANES:SUBLANES + tm, :] = zr
    prev = zbuf[SUBLANES - 1:SUBLANES - 1 + tm, :]
    z = zr + (prev - zr) * mu_ref[...]
    zbuf[0:SUBLANES, :] = zbuf[tm:tm + SUBLANES, :]

    r = z[:, 0:rw]
    k = z[:, rw:2 * rw]
    v = z[:, 2 * rw:3 * rw]
    lwa = z[:, 3 * rw:3 * rw + LANES]
    lg = z[:, 3 * rw + LANES:3 * rw + 2 * LANES]
    w0, a0, k_k, k_a, r_k, gn_g, gn_b, v0 = (vec_ref[j:j + 1, :] for j in range(8))

    lane = lax.broadcasted_iota(jnp.int32, (1, LANES), 1)
    lora_in = jnp.where(lane < LANES // 2, jnp.tanh(lwa), lwa)
    wl = _dot(lora_in, wa2_ref[...])
    lwd = -DECAY_SCALE * _sigmoid(w0 + wl[:, 0:rw])
    a = _sigmoid(a0 + wl[:, rw:2 * rw])
    g_s[...] = _dot(_sigmoid(lg), g2_ref[...])
    if has_vres:
        lv = z[:, 3 * rw + 2 * LANES:3 * rw + 3 * LANES]
        v = v + (vf_ref[...] - v) * _sigmoid(v0 + _dot(lv, v2_ref[...]))
    else:
        vfo_ref[...] = v
    kk = k * k_k
    kkn = kk * jnp.minimum(lax.rsqrt(_head_sum(kk * kk, head)), 1.0 / KK_NORM_FLOOR)
    kp = k * (1.0 + (a - 1.0) * k_a)
    bonus_s[...] = _head_sum(r * kp * r_k, head) * v
    r_s[...] = r
    lw_s[...] = lwd
    k_s[...] = kp
    v_s[...] = v
    a_s[...] = -kkn
    b_s[...] = kkn * a

    L = CHUNK
    ri = lax.broadcasted_iota(jnp.int32, (L, L), 0)
    ci = lax.broadcasted_iota(jnp.int32, (L, L), 1)
    tri = jnp.where(ri >= ci, 1.0, 0.0).astype(BF16)
    crow = lax.broadcasted_iota(jnp.int32, (L, gw), 0)
    ccol = lax.broadcasted_iota(jnp.int32, (L, gw), 1) % L
    cmask_s = crow > ccol
    cmask_i = crow >= ccol
    eye_c = crow == ccol
    eye_f = jnp.where(eye_c, 1.0, 0.0)
    glane = lax.broadcasted_iota(jnp.int32, (1, gw), 1)
    head_masks = [(glane >= h * head) & (glane < (h + 1) * head) for h in range(HEAD_GROUP)]
    mid = L // 2 - 1
    n_dbl = int(math.log2(L))

    def stack(x):
        xb = x.astype(BF16)
        zero = jnp.zeros_like(xb)
        return jnp.concatenate([jnp.where(m, xb, zero) for m in head_masks], axis=0)

    n_chunk = tm // L
    decay = []
    for c in range(n_chunk):
        rows = slice(c * L, (c + 1) * L)
        lw_c = lw_s[rows, :]
        hi, md, _ = _split3(lw_c)
        dd = lambda a_, b_: jnp.dot(a_, b_, preferred_element_type=F32)
        cl = dd(tri, hi) + dd(tri, md)
        c_mid = cl[mid:mid + 1, :]
        c_end = cl[L - 1:L, :]
        decay.append(dict(w_inc=jnp.exp(cl - c_mid), w_exc=jnp.exp(cl - lw_c - c_mid),
                          w_inv=jnp.exp(c_mid - cl), w_end=jnp.exp(c_end - cl),
                          e0=jnp.exp(c_mid), wl_end=jnp.exp(c_end)))

    inst = [dict(c=c, gi=gi, rows=slice(c * L, (c + 1) * L), ls=slice(gi * gw, (gi + 1) * gw))
            for c in range(n_chunk) for gi in range(n_grp)]

    for st in inst:
        rows, ls, dc = st["rows"], st["ls"], decay[st["c"]]
        st["e0"] = dc["e0"][:, ls]
        st["wl_end"] = dc["wl_end"][:, ls]
        a_t = a_s[rows, ls] * dc["w_exc"][:, ls]
        st["r_t"] = r_s[rows, ls] * dc["w_inc"][:, ls]
        b_t = b_s[rows, ls] * dc["w_inv"][:, ls]
        k_t = k_s[rows, ls] * dc["w_inv"][:, ls]
        b_h = b_s[rows, ls] * dc["w_end"][:, ls]
        k_h = k_s[rows, ls] * dc["w_end"][:, ls]
        st["sv"] = stack(v_s[rows, ls])
        st["sa"] = stack(a_t)
        st["sbk_h"] = jnp.concatenate([stack(b_h), stack(k_h)], axis=0)
        m4 = _dot_nt(jnp.concatenate([a_t, st["r_t"]], axis=0),
                     jnp.concatenate([stack(b_t), stack(k_t)], axis=0))
        st["c_ab"] = jnp.where(cmask_s, m4[0:L, 0:gw], 0.0)
        st["c_ak"] = jnp.where(cmask_s, m4[0:L, gw:2 * gw], 0.0)
        st["c_rb"] = jnp.where(cmask_i, m4[L:2 * L, 0:gw], 0.0)
        st["c_rk"] = jnp.where(cmask_i, m4[L:2 * L, gw:2 * gw], 0.0)

    for st in inst:
        st["s_pow"] = _dot(st["c_ab"], stack(st["c_ab"]))
        st["t_c"] = eye_f + st["c_ab"]
        st["v2"] = _dot(st["c_ak"], st["sv"])
        st["ct"] = _dot_nt(eye_f, st["sbk_h"])

    for it in range(1, n_dbl):
        for st in inst:
            bd = stack(st["s_pow"])
            if it < n_dbl - 1:
                prod = _dot(jnp.concatenate([st["t_c"], st["s_pow"]], axis=0), bd)
                st["t_c"] = st["t_c"] + prod[0:L]
                st["s_pow"] = prod[L:2 * L]
            else:
                st["t_c"] = st["t_c"] + _dot(st["t_c"], bd)

    for st in inst:
        x2 = _dot(st["t_c"], jnp.concatenate([st["sa"], stack(st["v2"])], axis=1))
        st["s_x2"] = jnp.concatenate([stack(x2[:, 0:gw]), stack(x2[:, gw:2 * gw])], axis=1)

    for st in inst:
        w1 = _dot(st["c_rb"], st["s_x2"])
        st["rhat"] = (st["r_t"] + w1[:, 0:gw]) * st["e0"]
        st["y0"] = w1[:, gw:2 * gw] + _dot(st["c_rk"], st["sv"])
        pq = _dot(st["ct"][:, 0:gw], st["s_x2"])
        st["p_c"] = jnp.where(eye_c, st["wl_end"], 0.0) + pq[:, 0:gw] * st["e0"]
        st["q_c"] = pq[:, gw:2 * gw] + _dot(st["ct"][:, gw:2 * gw], st["sv"])

    for st in inst:
        gi = st["gi"]
        sh = stack(h_s[gi])
        y_s[st["rows"], st["ls"]] = _dot(st["rhat"], sh) + st["y0"]
        h_s[gi] = _dot(st["p_c"], sh) + st["q_c"]

    y = y_s[...]
    mu = _head_sum(y, head) * (1.0 / head)
    dlt = y - mu
    var = _head_sum(dlt * dlt, head) * (1.0 / head)
    yn = dlt * lax.rsqrt(var + RW_GN_EPS) * gn_g + gn_b
    y_rw = _dot((yn + bonus_s[...]) * g_s[...], wo_ref[...])
    merged = _sigmoid(g0_ref[...].astype(F32)) * y_rw + m12_ref[...]
    o = _dot(merged, wout_ref[...])
    gt = ada_ref[2:3, :]
    o_ref[...] = _layernorm(alpha * x_ref[...] + gt * o, lnv_ref[0:1, :], lnv_ref[1:2, :], LN_EPS)


def _rwkv_call(z, g, m12, x, ada, vfirst, mu, wa2, g2, v2, vecs, wo, wout, lnv, tm, head, alpha):
    bsz, s, d = x.shape
    zw = z.shape[-1]
    rw = wo.shape[0]
    has_vres = vfirst is not None
    c2 = lambda b, i: (0, 0)
    tile = lambda w: pl.BlockSpec((None, tm, w), lambda b, i: (b, i, 0))
    in_specs = [tile(zw), tile(d), tile(d), tile(d),
                pl.BlockSpec((None,) + ada.shape[1:], lambda b, i: (b, 0, 0))]
    args = [z, g, m12, x, ada]
    if has_vres:
        in_specs.append(tile(rw))
        args.append(vfirst)
    in_specs += [pl.BlockSpec(mu.shape, c2), pl.BlockSpec(wa2.shape, c2), pl.BlockSpec(g2.shape, c2)]
    args += [mu, wa2, g2]
    if has_vres:
        in_specs.append(pl.BlockSpec(v2.shape, c2))
        args.append(v2)
    in_specs += [pl.BlockSpec(vecs.shape, c2), pl.BlockSpec(wo.shape, c2), pl.BlockSpec(wout.shape, c2),
                 pl.BlockSpec(lnv.shape, c2)]
    args += [vecs, wo, wout, lnv]
    out_specs = [tile(d)]
    out_shape = [jax.ShapeDtypeStruct((bsz, s, d), F32)]
    if not has_vres:
        out_specs.append(tile(rw))
        out_shape.append(jax.ShapeDtypeStruct((bsz, s, rw), F32))
    gw = HEAD_GROUP * head
    scratch = [pltpu.VMEM((tm + SUBLANES, zw), F32)]
    scratch += [pltpu.VMEM((tm, rw), F32) for _ in range(9)]
    scratch += [pltpu.VMEM((rw // gw, CHUNK, gw), F32)]
    kern = functools.partial(_rwkv_kernel, tm=tm, rw=rw, head=head, has_vres=has_vres, alpha=alpha)
    res = pl.pallas_call(
        kern,
        grid=(bsz, s // tm),
        in_specs=in_specs,
        out_specs=out_specs,
        out_shape=out_shape,
        scratch_shapes=scratch,
        compiler_params=pltpu.CompilerParams(
            dimension_semantics=("parallel", "arbitrary"), vmem_limit_bytes=VMEM_LIMIT_BYTES),
        name="rwkv_vres" if has_vres else "rwkv",
    )(*args)
    return (res[0], vfirst) if has_vres else (res[0], res[1])


def _mlp_kernel(x_ref, ada_ref, w1_ref, w2_ref, lnv_ref, o_ref, *, alpha, n_split):
    x = x_ref[...]
    sh = ada_ref[3:4, :]
    sc = ada_ref[4:5, :]
    gt = ada_ref[5:6, :]
    h = (x * (1.0 + sc) + sh).astype(BF16)
    dff = w1_ref.shape[1]
    step = dff // n_split
    acc = None
    for j in range(n_split):
        t = jnp.dot(h, w1_ref[:, j * step:(j + 1) * step], preferred_element_type=F32)
        t = jnp.square(jnp.maximum(t, 0.0)).astype(BF16)
        part = jnp.dot(t, w2_ref[j * step:(j + 1) * step, :], preferred_element_type=F32)
        acc = part if acc is None else acc + part
    o_ref[...] = _layernorm(alpha * x + gt * acc, lnv_ref[0:1, :], lnv_ref[1:2, :], LN_EPS)


def _mlp_call(x, ada, w1, w2, lnv, tm, alpha):
    bsz, s, d = x.shape
    dff = w1.shape[1]
    c2 = lambda b, i: (0, 0)
    kern = functools.partial(_mlp_kernel, alpha=alpha, n_split=dff // d)
    return pl.pallas_call(
        kern,
        grid=(bsz, s // tm),
        in_specs=[
            pl.BlockSpec((None, tm, d), lambda b, i: (b, i, 0)),
            pl.BlockSpec((None,) + ada.shape[1:], lambda b, i: (b, 0, 0)),
            pl.BlockSpec(w1.shape, c2, pipeline_mode=pl.Buffered(1)),
            pl.BlockSpec(w2.shape, c2, pipeline_mode=pl.Buffered(1)),
            pl.BlockSpec(lnv.shape, c2),
        ],
        out_specs=pl.BlockSpec((None, tm, d), lambda b, i: (b, i, 0)),
        out_shape=jax.ShapeDtypeStruct((bsz, s, d), F32),
        compiler_params=pltpu.CompilerParams(
            dimension_semantics=("parallel", "arbitrary"), vmem_limit_bytes=VMEM_LIMIT_BYTES),
        name="mlp",
    )(x, ada, w1, w2, lnv)


def _pick_tile(s, pref):
    tm = min(pref, s)
    assert s % tm == 0 and tm % CHUNK == 0, (s, tm)
    return tm


def kernel(x, c, ada_w, ada_b, w_in, w_in_vres, shift_mu, shift_mu_vres, rw_w0, rw_w2, rw_a0, rw_a2, rw_g2, rw_v0, rw_v2, rw_kk, rw_ka, rw_rk, rw_gn_g, rw_gn_b, rw_wo, cv_w, cv_b, cv_ln_g, cv_ln_b, cv_wo, pl_w, pl_scale, pl_wo, w_out, ln_m_g, ln_m_b, mlp_w1, mlp_w2, ln_f_g, ln_f_b):
    bsz, s, d = x.shape
    depth = ada_w.shape[0]
    heads, head = rw_rk.shape[1], rw_rk.shape[2]
    rw = heads * head
    lw_w, la_w, lg_w = rw_w2.shape[1], rw_a2.shape[1], rw_g2.shape[1]
    lv_w = rw_v2.shape[1]
    cv = cv_w.shape[2]
    pw = pl_scale.shape[1]
    assert lw_w + la_w == LANES and lg_w == LANES and lv_w <= LANES and LANES % head == 0
    assert rw % (HEAD_GROUP * head) == 0 and head == CHUNK
    alpha = float((2 * depth) ** 0.25)
    rw_shift = 3 * rw + lw_w + la_w + lg_w
    cv_off, pl_off = rw_shift, rw_shift + 2 * cv
    gt_off = pl_off + pw

    tm_in = _pick_tile(s, 512)
    tm_rw = _pick_tile(s, 512)
    tm_ff = _pick_tile(s, 1024)

    ada = _ada_call(c, ada_w, ada_b)

    v_first = None
    for l in range(depth):
        has_vres = l > 0
        w_l = w_in[l]
        wz = w_l[:, :rw_shift]
        mu = shift_mu[l]
        if has_vres:
            pad = LANES - lv_w
            wz = jnp.concatenate([wz, w_in_vres[l - 1], jnp.zeros((d, pad), F32)], axis=1)
            mu = jnp.concatenate([mu, shift_mu_vres[l - 1], jnp.zeros((pad,), F32)])
        wz, wu, wp, wg = (t.astype(BF16) for t in (wz, w_l[:, cv_off:pl_off], w_l[:, pl_off:gt_off], w_l[:, gt_off:]))
        z, g, m12 = _front_call(x, ada[l], wz, wu, wp, wg, cv_w[l], cv_b[l], cv_ln_g[l], cv_ln_b[l],
                                cv_wo[l].astype(BF16), pl_w[l].astype(BF16), pl_scale[l], pl_wo[l].astype(BF16),
                                tm_in)

        wa2 = jnp.zeros((LANES, 2 * rw), F32)
        wa2 = wa2.at[:lw_w, :rw].set(rw_w2[l]).at[lw_w:, rw:].set(rw_a2[l]).astype(BF16)
        v0 = rw_v0[l - 1] if has_vres else jnp.zeros((rw,), F32)
        vecs = jnp.stack([rw_w0[l], rw_a0[l], rw_kk[l], rw_ka[l], rw_rk[l].reshape(rw),
                          rw_gn_g[l], rw_gn_b[l], v0])
        v2 = None
        if has_vres:
            v2 = jnp.zeros((LANES, rw), F32).at[:lv_w].set(rw_v2[l - 1]).astype(BF16)
        lnm = jnp.stack([ln_m_g[l], ln_m_b[l]])
        x, v_first = _rwkv_call(z, g, m12, x, ada[l], v_first, mu.reshape(1, -1), wa2, rw_g2[l].astype(BF16), v2,
                                vecs, rw_wo[l].astype(BF16), w_out[l].astype(BF16), lnm, tm_rw, head, alpha)

        lnf = jnp.stack([ln_f_g[l], ln_f_b[l]])
        x = _mlp_call(x, ada[l], mlp_w1[l].astype(BF16), mlp_w2[l].astype(BF16), lnf, tm_ff, alpha)
    return x
```

```python
import functools
import math

import jax
import jax.numpy as jnp
from jax import lax
from jax.experimental import pallas as pl
from jax.experimental.pallas import tpu as pltpu

F32 = jnp.float32
BF16 = jnp.bfloat16

LN_EPS = 1e-5
RW_GN_EPS = 64e-5
POOL_WINDOWS = (2, 4, 8, 16)
KK_NORM_FLOOR = 1e-12
DECAY_SCALE = math.exp(-0.5)

LANES = 128
SUBLANES = 8
VMEM_LIMIT_BYTES = 56 * 1024 * 1024

CHUNK = 64
HEAD_GROUP = 4


def _dot(a, b):
    return jnp.dot(a.astype(BF16), b.astype(BF16), preferred_element_type=F32)


def _dot_nt(a, b):
    return lax.dot_general(a.astype(BF16), b.astype(BF16), (((1,), (1,)), ((), ())),
                           preferred_element_type=F32)


def _dot_tn(a, b):
    return lax.dot_general(a.astype(BF16), b.astype(BF16), (((0,), (0,)), ((), ())),
                           preferred_element_type=F32)


def _split3(x):
    hi = x.astype(BF16)
    r1 = x - hi.astype(F32)
    mid = r1.astype(BF16)
    lo = (r1 - mid.astype(F32)).astype(BF16)
    return hi, mid, lo


def _layernorm(x, g, b, eps):
    mu = jnp.mean(x, axis=-1, keepdims=True)
    d = x - mu
    var = jnp.mean(d * d, axis=-1, keepdims=True)
    return d * lax.rsqrt(var + eps) * g + b


def _sigmoid(x):
    return 0.5 * jnp.tanh(0.5 * x) + 0.5


def _ada_kernel(c_ref, w_ref, b_ref, o_ref):
    c = c_ref[...]
    cond = c * _sigmoid(c)
    c_hi, c_mid, c_lo = _split3(cond)
    w_hi, w_mid, w_lo = _split3(w_ref[...])
    dd = lambda a, b: jnp.dot(a, b, preferred_element_type=F32)
    acc = dd(c_hi, w_hi) + (dd(c_hi, w_mid) + dd(c_mid, w_hi)) + (dd(c_hi, w_lo) + dd(c_mid, w_mid) + dd(c_lo, w_hi))
    o_ref[...] = acc + b_ref[...]


def _ada_call(c, ada_w, ada_b):
    depth, d, nd = ada_w.shape
    n_ada = nd // d
    bsz = c.shape[0]
    out = pl.pallas_call(
        _ada_kernel,
        grid=(depth, n_ada),
        in_specs=[
            pl.BlockSpec((bsz, d), lambda l, j: (0, 0)),
            pl.BlockSpec((None, d, d), lambda l, j: (l, 0, j)),
            pl.BlockSpec((None, None, 1, d), lambda l, j: (l, j, 0, 0)),
        ],
        out_specs=pl.BlockSpec((None, None, bsz, d), lambda l, j: (l, j, 0, 0)),
        out_shape=jax.ShapeDtypeStruct((depth, n_ada, bsz, d), F32),
        name="ada",
    )(c, ada_w, ada_b.reshape(depth, n_ada, 1, d))
    return jnp.transpose(out, (0, 2, 1, 3))


CONV_ROW_BLOCK = 128
PROJ_COL_CHUNK = 512


def _front_kernel(x_ref, ada_ref, wz_ref, wu_ref, wp_ref, wg_ref, cvw_ref, cvb_ref, lng_ref, lnb_ref, cvwo_ref,
                  plw_ref, pls_ref, plwo_ref, z_ref, g0_ref, m12_ref, hbuf, shbuf, pbuf, acc_s, gate_s,
                  *, tm, cv, conv_width, pg):
    i = pl.program_id(1)
    d = g0_ref.shape[1]
    halo_c = hbuf.shape[0] - tm
    halo_p = pbuf.shape[0] - tm

    @pl.when(i == 0)
    def _():
        hbuf[0:halo_c, :] = jnp.zeros((halo_c, cv), F32)
        pbuf[0:halo_p, :] = jnp.zeros((halo_p, pbuf.shape[1]), F32)

    sh = ada_ref[0:1, :]
    sc = ada_ref[1:2, :]
    h = (x_ref[...] * (1.0 + sc) + sh).astype(BF16)

    u = jnp.dot(h, wu_ref[...], preferred_element_type=F32)
    hbuf[halo_c:halo_c + tm, :] = u[:, :cv] * _sigmoid(u[:, cv:])
    pf = jnp.dot(h, wp_ref[...], preferred_element_type=F32)
    pbuf[halo_p:halo_p + tm, :] = pf

    def z_job(c0, c1):
        def run():
            z_ref[:, c0:c1] = jnp.dot(h, wz_ref[:, c0:c1], preferred_element_type=F32).astype(z_ref.dtype)
        return run

    def g_job(c0, c1):
        def run():
            gv = jnp.dot(h, wg_ref[:, c0:c1], preferred_element_type=F32)
            if c0 < d:
                g0_ref[:, c0:c1] = gv.astype(g0_ref.dtype)
            else:
                gate_s[:, c0 - d:c1 - d] = _sigmoid(gv)
        return run

    zw = z_ref.shape[1]
    jobs = [z_job(c0, min(c0 + PROJ_COL_CHUNK, zw)) for c0 in range(0, zw, PROJ_COL_CHUNK)]
    jobs += [g_job(c0, c0 + PROJ_COL_CHUNK) for c0 in range(0, wg_ref.shape[1], PROJ_COL_CHUNK)]

    base = halo_c - (conv_width - 1)
    span = shbuf.shape[2]
    n_blocks = (cv // LANES) * (tm // CONV_ROW_BLOCK)
    n_jobs = len(jobs)
    n_done = 0
    for sl in range(cv // LANES):
        cs = slice(sl * LANES, (sl + 1) * LANES)
        sb = shbuf.at[sl % 2]
        for r in range(1, SUBLANES):
            sb[r - 1] = hbuf[r:r + span, cs]
        for rb in range(tm // CONV_ROW_BLOCK):
            r0 = rb * CONV_ROW_BLOCK
            acc = None
            for j in range(conv_width):
                q, r = divmod(base + j, SUBLANES)
                lo = q * SUBLANES + r0
                src = hbuf[lo:lo + CONV_ROW_BLOCK, cs] if r == 0 else sb[r - 1, lo:lo + CONV_ROW_BLOCK, :]
                term = src * cvw_ref[j:j + 1, cs]
                acc = term if acc is None else acc + term
            acc_s[r0:r0 + CONV_ROW_BLOCK, cs] = acc + cvb_ref[:, cs]
            n_done += 1
            while jobs and (n_jobs - len(jobs)) * n_blocks < n_done * n_jobs:
                jobs.pop(0)()
    assert not jobs
    hbuf[0:halo_c, :] = hbuf[tm:tm + halo_c, :]

    hn = _layernorm(acc_s[...], lng_ref[...], lnb_ref[...], LN_EPS)
    hn = hn * _sigmoid(hn)
    y_cv = _dot(hn, cvwo_ref[...])

    t1 = (lax.broadcasted_iota(jnp.int32, (tm, 1), 0) + (i * tm + 1)).astype(F32)
    mixed = []
    for gi, win in enumerate(POOL_WINDOWS):
        sl = slice(gi * pg, (gi + 1) * pg)
        ssum = pbuf[halo_p:halo_p + tm, sl]
        for dlt in range(1, win):
            ssum = ssum + pbuf[halo_p - dlt:halo_p - dlt + tm, sl]
        pooled = ssum / jnp.minimum(t1, float(win)) - pbuf[halo_p:halo_p + tm, sl]
        mixed.append(_dot(pooled, plw_ref[gi]))
    mixed = jnp.concatenate(mixed, axis=1) * pls_ref[...]
    y_pl = _dot(mixed, plwo_ref[...])
    pbuf[0:halo_p, :] = pbuf[tm:tm + halo_p, :]

    m12_ref[...] = gate_s[:, 0:d] * y_cv + gate_s[:, d:2 * d] * y_pl


def _front_call(x, ada, wz, wu, wp, wg, cvw, cvb, lng, lnb, cvwo, plw, pls, plwo, tm):
    bsz, s, d = x.shape
    conv_width, cv = cvw.shape
    pw = wp.shape[1]
    pg = plw.shape[1]
    zw = wz.shape[1]
    halo_c = -(-(conv_width - 1) // SUBLANES) * SUBLANES
    halo_p = -(-(max(POOL_WINDOWS) - 1) // SUBLANES) * SUBLANES
    c2 = lambda b, i: (0, 0)
    tile = lambda w: pl.BlockSpec((None, tm, w), lambda b, i: (b, i, 0))
    resident = lambda a: pl.BlockSpec(a.shape, c2, pipeline_mode=pl.Buffered(1))
    kern = functools.partial(_front_kernel, tm=tm, cv=cv, conv_width=conv_width, pg=pg)
    return pl.pallas_call(
        kern,
        grid=(bsz, s // tm),
        in_specs=[
            tile(d),
            pl.BlockSpec((None,) + ada.shape[1:], lambda b, i: (b, 0, 0)),
            resident(wz), resident(wu), resident(wp), resident(wg),
            pl.BlockSpec(cvw.shape, c2),
            pl.BlockSpec((1, cv), c2),
            pl.BlockSpec((1, cv), c2),
            pl.BlockSpec((1, cv), c2),
            resident(cvwo),
            pl.BlockSpec(plw.shape, lambda b, i: (0, 0, 0)),
            pl.BlockSpec((1, pw), c2),
            resident(plwo),
        ],
        out_specs=[tile(zw), tile(d), tile(d)],
        out_shape=[jax.ShapeDtypeStruct((bsz, s, zw), BF16), jax.ShapeDtypeStruct((bsz, s, d), BF16),
                   jax.ShapeDtypeStruct((bsz, s, d), F32)],
        scratch_shapes=[pltpu.VMEM((tm + halo_c, cv), F32),
                        pltpu.VMEM((2, SUBLANES - 1, tm + halo_c - SUBLANES, LANES), F32),
                        pltpu.VMEM((tm + halo_p, pw), F32),
                        pltpu.VMEM((tm, cv), F32),
                        pltpu.VMEM((tm, 2 * d), F32)],
        compiler_params=pltpu.CompilerParams(
            dimension_semantics=("parallel", "arbitrary"), vmem_limit_bytes=VMEM_LIMIT_BYTES),
        name="front",
    )(x, ada, wz, wu, wp, wg, cvw, cvb.reshape(1, cv), lng.reshape(1, cv), lnb.reshape(1, cv), cvwo,
      plw, pls.reshape(1, pw), plwo)


def _head_sum(x, head):
    per_slab = LANES // head
    lane = lax.broadcasted_iota(jnp.int32, (1, LANES), 1)
    out = []
    for s0 in range(0, x.shape[1], LANES):
        slab = x[:, s0:s0 + LANES]
        res = jnp.zeros_like(slab)
        for q in range(per_slab):
            m = (lane >= q * head) & (lane < (q + 1) * head)
            tot = jnp.sum(jnp.where(m, slab, 0.0), axis=-1, keepdims=True)
            res = jnp.where(m, tot, res)
        out.append(res)
    return jnp.concatenate(out, axis=1)


def _rwkv_kernel(*refs, tm, rw, head, has_vres):
    if has_vres:
        (z_ref, zn_ref, vf_ref, vfn_ref, mu_ref, wa2_ref, g2_ref, v2_ref, vec_ref, yg_ref,
         zbuf, r_s, lw_s, k_s, v_s, a_s, b_s, g_s, bonus_s, y_s, h_s) = refs
        vfo_ref = None
    else:
        (z_ref, zn_ref, mu_ref, wa2_ref, g2_ref, vec_ref, yg_ref, vfo_ref,
         zbuf, r_s, lw_s, k_s, v_s, a_s, b_s, g_s, bonus_s, y_s, h_s) = refs
        vf_ref = vfn_ref = v2_ref = None
    i = pl.program_id(1)
    cur = i % 2
    nxt = 1 - cur
    gw = HEAD_GROUP * head
    n_grp = rw // gw
    L = CHUNK
    n_chunk = tm // L
    w0, a0, k_k, k_a, r_k, gn_g, gn_b, v0 = (vec_ref[j:j + 1, :] for j in range(8))
    lane = lax.broadcasted_iota(jnp.int32, (1, LANES), 1)

    def prepare_rows(zsrc, vfsrc, slot, p):
        rows = slice(p * L, (p + 1) * L)
        brow = slice(SUBLANES + p * L, SUBLANES + (p + 1) * L)
        prow = slice(SUBLANES - 1 + p * L, SUBLANES - 1 + (p + 1) * L)

        def shifted(c0, c1):
            zr = zsrc[rows, c0:c1].astype(F32)
            zbuf[brow, c0:c1] = zr
            return zr + (zbuf[prow, c0:c1] - zr) * mu_ref[:, c0:c1]

        lwa = shifted(3 * rw, 3 * rw + LANES)
        lora_in = jnp.where(lane < LANES // 2, jnp.tanh(lwa), lwa)
        wl = _dot(lora_in, wa2_ref[...])
        yield
        lw_s[slot, rows, :] = -DECAY_SCALE * _sigmoid(w0 + wl[:, 0:rw])
        yield
        a = _sigmoid(a0 + wl[:, rw:2 * rw])
        yield
        lg = shifted(3 * rw + LANES, 3 * rw + 2 * LANES)
        g_s[slot, rows, :] = _dot(_sigmoid(lg), g2_ref[...])
        yield
        v = shifted(2 * rw, 3 * rw)
        yield
        if has_vres:
            lv = shifted(3 * rw + 2 * LANES, 3 * rw + 3 * LANES)
            v = v + (vfsrc[rows, :] - v) * _sigmoid(v0 + _dot(lv, v2_ref[...]))
            yield
        v_s[slot, rows, :] = v
        k = shifted(rw, 2 * rw)
        yield
        kk = k * k_k
        n2 = _head_sum(kk * kk, head)
        yield
        kkn = kk * jnp.minimum(lax.rsqrt(n2), 1.0 / KK_NORM_FLOOR)
        a_s[slot, rows, :] = -kkn
        yield
        b_s[slot, rows, :] = kkn * a
        kp = k * (1.0 + (a - 1.0) * k_a)
        k_s[slot, rows, :] = kp
        yield
        r = shifted(0, rw)
        r_s[slot, rows, :] = r
        yield
        bsum = _head_sum(r * kp * r_k, head)
        yield
        bonus_s[slot, rows, :] = bsum * v

    @pl.when(i == 0)
    def _():
        h_s[...] = jnp.zeros(h_s.shape, F32)
        zbuf[0:SUBLANES, :] = jnp.zeros((SUBLANES, zbuf.shape[1]), F32)
        for p in range(n_chunk):
            for _ in prepare_rows(z_ref, vf_ref, cur, p):
                pass

    zbuf[0:SUBLANES, :] = z_ref[tm - SUBLANES:tm, :].astype(F32)

    def prepare_next():
        for p in range(n_chunk):
            yield from prepare_rows(zn_ref, vfn_ref, nxt, p)

    prepare_steps = prepare_next()

    def prepare_some(n=1):
        for _ in range(n):
            next(prepare_steps, None)

    ri = lax.broadcasted_iota(jnp.int32, (L, L), 0)
    ci = lax.broadcasted_iota(jnp.int32, (L, L), 1)
    tri = jnp.where(ri >= ci, 1.0, 0.0).astype(BF16)
    crow = lax.broadcasted_iota(jnp.int32, (L, gw), 0)
    ccol = lax.broadcasted_iota(jnp.int32, (L, gw), 1) % L
    cmask_s = crow > ccol
    cmask_i = crow >= ccol
    eye_c = crow == ccol
    eye_f = jnp.where(eye_c, 1.0, 0.0)
    glane = lax.broadcasted_iota(jnp.int32, (1, gw), 1)
    head_masks = [(glane >= h * head) & (glane < (h + 1) * head) for h in range(HEAD_GROUP)]
    mid = L // 2 - 1
    n_dbl = int(math.log2(L))

    def stack(x):
        xb = x.astype(BF16)
        zero = jnp.zeros_like(xb)
        return jnp.concatenate([jnp.where(m, xb, zero) for m in head_masks], axis=0)

    decay = []
    for c in range(n_chunk):
        rows = slice(c * L, (c + 1) * L)
        lw_c = lw_s[cur, rows, :]
        hi, md, _ = _split3(lw_c)
        dd = lambda a_, b_: jnp.dot(a_, b_, preferred_element_type=F32)
        cl = dd(tri, hi) + dd(tri, md)
        c_mid = cl[mid:mid + 1, :]
        c_end = cl[L - 1:L, :]
        decay.append(dict(w_inc=jnp.exp(cl - c_mid), w_exc=jnp.exp(cl - lw_c - c_mid),
                          w_inv=jnp.exp(c_mid - cl), w_end=jnp.exp(c_end - cl),
                          e0=jnp.exp(c_mid), wl_end=jnp.exp(c_end)))

    inst = [dict(c=c, gi=gi, rows=slice(c * L, (c + 1) * L), ls=slice(gi * gw, (gi + 1) * gw))
            for c in range(n_chunk) for gi in range(n_grp)]

    for st in inst:
        rows, ls, dc = st["rows"], st["ls"], decay[st["c"]]
        st["e0"] = dc["e0"][:, ls]
        st["wl_end"] = dc["wl_end"][:, ls]
        a_t = a_s[cur, rows, ls] * dc["w_exc"][:, ls]
        st["r_t"] = r_s[cur, rows, ls] * dc["w_inc"][:, ls]
        b_t = b_s[cur, rows, ls] * dc["w_inv"][:, ls]
        k_t = k_s[cur, rows, ls] * dc["w_inv"][:, ls]
        b_h = b_s[cur, rows, ls] * dc["w_end"][:, ls]
        k_h = k_s[cur, rows, ls] * dc["w_end"][:, ls]
        st["sv"] = stack(v_s[cur, rows, ls])
        st["sa"] = stack(a_t)
        st["sbk_h"] = jnp.concatenate([stack(b_h), stack(k_h)], axis=0)
        m4 = _dot_nt(jnp.concatenate([a_t, st["r_t"]], axis=0),
                     jnp.concatenate([stack(b_t), stack(k_t)], axis=0))
        st["c_ab"] = jnp.where(cmask_s, m4[0:L, 0:gw], 0.0)
        st["c_ak"] = jnp.where(cmask_s, m4[0:L, gw:2 * gw], 0.0)
        st["c_rb"] = jnp.where(cmask_i, m4[L:2 * L, 0:gw], 0.0)
        st["c_rk"] = jnp.where(cmask_i, m4[L:2 * L, gw:2 * gw], 0.0)
        prepare_some()

    for st in inst:
        st["s_pow"] = _dot(st["c_ab"], stack(st["c_ab"]))
        st["t_c"] = eye_f + st["c_ab"]
        st["v2"] = _dot(st["c_ak"], st["sv"])
        st["ct"] = _dot_nt(eye_f, st["sbk_h"])
        prepare_some()

    for it in range(1, n_dbl):
        for st in inst:
            bd = stack(st["s_pow"])
            if it < n_dbl - 1:
                prod = _dot(jnp.concatenate([st["t_c"], st["s_pow"]], axis=0), bd)
                st["t_c"] = st["t_c"] + prod[0:L]
                st["s_pow"] = prod[L:2 * L]
            else:
                st["t_c"] = st["t_c"] + _dot(st["t_c"], bd)
            prepare_some()

    for st in inst:
        x2 = _dot(st["t_c"], jnp.concatenate([st["sa"], stack(st["v2"])], axis=1))
        st["s_x2"] = jnp.concatenate([stack(x2[:, 0:gw]), stack(x2[:, gw:2 * gw])], axis=1)
        prepare_some()
    for _ in prepare_steps:
        pass

    for st in inst:
        w1 = _dot(st["c_rb"], st["s_x2"])
        st["rhat"] = (st["r_t"] + w1[:, 0:gw]) * st["e0"]
        st["y0"] = w1[:, gw:2 * gw] + _dot(st["c_rk"], st["sv"])
        pq = _dot(st["ct"][:, 0:gw], st["s_x2"])
        st["p_c"] = jnp.where(eye_c, st["wl_end"], 0.0) + pq[:, 0:gw] * st["e0"]
        st["q_c"] = pq[:, gw:2 * gw] + _dot(st["ct"][:, gw:2 * gw], st["sv"])

    for st in inst:
        gi = st["gi"]
        sh = stack(h_s[gi])
        y_s[st["rows"], st["ls"]] = _dot(st["rhat"], sh) + st["y0"]
        h_s[gi] = _dot(st["p_c"], sh) + st["q_c"]

    y = y_s[...]
    mu = _head_sum(y, head) * (1.0 / head)
    dlt = y - mu
    var = _head_sum(dlt * dlt, head) * (1.0 / head)
    yn = dlt * lax.rsqrt(var + RW_GN_EPS) * gn_g + gn_b
    yg_ref[...] = ((yn + bonus_s[cur]) * g_s[cur]).astype(yg_ref.dtype)
    if not has_vres:
        vfo_ref[...] = v_s[cur]


def _rwkv_call(z, vfirst, mu, wa2, g2, v2, vecs, tm, head):
    bsz, s, zw = z.shape
    rw = g2.shape[1]
    n_tiles = s // tm
    has_vres = vfirst is not None
    c2 = lambda b, i: (0, 0)
    tile = lambda w: pl.BlockSpec((None, tm, w), lambda b, i: (b, i, 0))
    next_tile = lambda w: pl.BlockSpec((None, tm, w), lambda b, i: (b, jnp.minimum(i + 1, n_tiles - 1), 0))
    in_specs = [tile(zw), next_tile(zw)]
    args = [z, z]
    if has_vres:
        in_specs += [tile(rw), next_tile(rw)]
        args += [vfirst, vfirst]
    in_specs += [pl.BlockSpec(mu.shape, c2), pl.BlockSpec(wa2.shape, c2), pl.BlockSpec(g2.shape, c2)]
    args += [mu, wa2, g2]
    if has_vres:
        in_specs.append(pl.BlockSpec(v2.shape, c2))
        args.append(v2)
    in_specs.append(pl.BlockSpec(vecs.shape, c2))
    args.append(vecs)
    out_specs = [tile(rw)]
    out_shape = [jax.ShapeDtypeStruct((bsz, s, rw), BF16)]
    if not has_vres:
        out_specs.append(tile(rw))
        out_shape.append(jax.ShapeDtypeStruct((bsz, s, rw), F32))
    gw = HEAD_GROUP * head
    scratch = [pltpu.VMEM((tm + SUBLANES, zw), F32)]
    scratch += [pltpu.VMEM((2, tm, rw), F32) for _ in range(8)]
    scratch += [pltpu.VMEM((tm, rw), F32)]
    scratch += [pltpu.VMEM((rw // gw, CHUNK, gw), F32)]
    kern = functools.partial(_rwkv_kernel, tm=tm, rw=rw, head=head, has_vres=has_vres)
    res = pl.pallas_call(
        kern,
        grid=(bsz, n_tiles),
        in_specs=in_specs,
        out_specs=out_specs,
        out_shape=out_shape,
        scratch_shapes=scratch,
        compiler_params=pltpu.CompilerParams(
            dimension_semantics=("parallel", "arbitrary"), vmem_limit_bytes=VMEM_LIMIT_BYTES),
        name="rwkv_vres" if has_vres else "rwkv",
    )(*args)
    return (res[0], vfirst) if has_vres else (res[0], res[1])


def _back_kernel(yg_ref, g0_ref, m12_ref, x_ref, ada_ref, wo_ref, wout_ref, lnm_ref, w1_ref, w2_ref, lnf_ref,
                 o_ref, *, alpha, n_split):
    y_rw = jnp.dot(yg_ref[...], wo_ref[...], preferred_element_type=F32)
    merged = _sigmoid(g0_ref[...].astype(F32)) * y_rw + m12_ref[...]
    o = _dot(merged, wout_ref[...])
    x = _layernorm(alpha * x_ref[...] + ada_ref[2:3, :] * o, lnm_ref[0:1, :], lnm_ref[1:2, :], LN_EPS)

    sh = ada_ref[3:4, :]
    sc = ada_ref[4:5, :]
    gt = ada_ref[5:6, :]
    h = (x * (1.0 + sc) + sh).astype(BF16)
    dff = w1_ref.shape[1]
    step = dff // n_split
    acc = None
    for j in range(n_split):
        t = jnp.dot(h, w1_ref[:, j * step:(j + 1) * step], preferred_element_type=F32)
        t = jnp.square(jnp.maximum(t, 0.0)).astype(BF16)
        part = jnp.dot(t, w2_ref[j * step:(j + 1) * step, :], preferred_element_type=F32)
        acc = part if acc is None else acc + part
    o_ref[...] = _layernorm(alpha * x + gt * acc, lnf_ref[0:1, :], lnf_ref[1:2, :], LN_EPS)


def _back_call(yg, g0, m12, x, ada, wo, wout, lnm, w1, w2, lnf, tm, alpha):
    bsz, s, d = x.shape
    dff = w1.shape[1]
    c2 = lambda b, i: (0, 0)
    tile = lambda w: pl.BlockSpec((None, tm, w), lambda b, i: (b, i, 0))
    resident = lambda a: pl.BlockSpec(a.shape, c2, pipeline_mode=pl.Buffered(1))
    kern = functools.partial(_back_kernel, alpha=alpha, n_split=dff // d)
    return pl.pallas_call(
        kern,
        grid=(bsz, s // tm),
        in_specs=[
            tile(yg.shape[-1]), tile(d), tile(d), tile(d),
            pl.BlockSpec((None,) + ada.shape[1:], lambda b, i: (b, 0, 0)),
            resident(wo), resident(wout), pl.BlockSpec(lnm.shape, c2),
            resident(w1), resident(w2), pl.BlockSpec(lnf.shape, c2),
        ],
        out_specs=tile(d),
        out_shape=jax.ShapeDtypeStruct((bsz, s, d), F32),
        compiler_params=pltpu.CompilerParams(
            dimension_semantics=("parallel", "arbitrary"), vmem_limit_bytes=VMEM_LIMIT_BYTES),
        name="back",
    )(yg, g0, m12, x, ada, wo, wout, lnm, w1, w2, lnf)


def _pick_tile(s, pref):
    tm = min(pref, s)
    assert s % tm == 0 and tm % CHUNK == 0, (s, tm)
    return tm


def kernel(x, c, ada_w, ada_b, w_in, w_in_vres, shift_mu, shift_mu_vres, rw_w0, rw_w2, rw_a0, rw_a2, rw_g2, rw_v0, rw_v2, rw_kk, rw_ka, rw_rk, rw_gn_g, rw_gn_b, rw_wo, cv_w, cv_b, cv_ln_g, cv_ln_b, cv_wo, pl_w, pl_scale, pl_wo, w_out, ln_m_g, ln_m_b, mlp_w1, mlp_w2, ln_f_g, ln_f_b):
    bsz, s, d = x.shape
    depth = ada_w.shape[0]
    heads, head = rw_rk.shape[1], rw_rk.shape[2]
    rw = heads * head
    lw_w, la_w, lg_w = rw_w2.shape[1], rw_a2.shape[1], rw_g2.shape[1]
    lv_w = rw_v2.shape[1]
    cv = cv_w.shape[2]
    pw = pl_scale.shape[1]
    assert lw_w + la_w == LANES and lg_w == LANES and lv_w <= LANES and LANES % head == 0
    assert rw % (HEAD_GROUP * head) == 0 and head == CHUNK
    alpha = float((2 * depth) ** 0.25)
    rw_shift = 3 * rw + lw_w + la_w + lg_w
    cv_off, pl_off = rw_shift, rw_shift + 2 * cv
    gt_off = pl_off + pw

    tm_in = _pick_tile(s, 512)
    tm_rw = _pick_tile(s, 512)
    tm_ff = _pick_tile(s, 512)

    ada = _ada_call(c, ada_w, ada_b)

    v_first = None
    for l in range(depth):
        has_vres = l > 0
        w_l = w_in[l]
        wz = w_l[:, :rw_shift]
        mu = shift_mu[l]
        if has_vres:
            pad = LANES - lv_w
            wz = jnp.concatenate([wz, w_in_vres[l - 1], jnp.zeros((d, pad), F32)], axis=1)
            mu = jnp.concatenate([mu, shift_mu_vres[l - 1], jnp.zeros((pad,), F32)])
        wz, wu, wp, wg = (t.astype(BF16) for t in (wz, w_l[:, cv_off:pl_off], w_l[:, pl_off:gt_off], w_l[:, gt_off:]))
        z, g, m12 = _front_call(x, ada[l], wz, wu, wp, wg, cv_w[l], cv_b[l], cv_ln_g[l], cv_ln_b[l],
                                cv_wo[l].astype(BF16), pl_w[l].astype(BF16), pl_scale[l], pl_wo[l].astype(BF16),
                                tm_in)

        wa2 = jnp.zeros((LANES, 2 * rw), F32)
        wa2 = wa2.at[:lw_w, :rw].set(rw_w2[l]).at[lw_w:, rw:].set(rw_a2[l]).astype(BF16)
        v0 = rw_v0[l - 1] if has_vres else jnp.zeros((rw,), F32)
        vecs = jnp.stack([rw_w0[l], rw_a0[l], rw_kk[l], rw_ka[l], rw_rk[l].reshape(rw),
                          rw_gn_g[l], rw_gn_b[l], v0])
        v2 = None
        if has_vres:
            v2 = jnp.zeros((LANES, rw), F32).at[:lv_w].set(rw_v2[l - 1]).astype(BF16)
        yg, v_first = _rwkv_call(z, v_first, mu.reshape(1, -1), wa2, rw_g2[l].astype(BF16), v2, vecs, tm_rw, head)

        lnm = jnp.stack([ln_m_g[l], ln_m_b[l]])
        lnf = jnp.stack([ln_f_g[l], ln_f_b[l]])
        x = _back_call(yg, g, m12, x, ada[l], rw_wo[l].astype(BF16), w_out[l].astype(BF16), lnm,
                       mlp_w1[l].astype(BF16), mlp_w2[l].astype(BF16), lnf, tm_ff, alpha)
    return x
```

```python
import functools
import math

import jax
import jax.numpy as jnp
from jax import lax
from jax.experimental import pallas as pl
from jax.experimental.pallas import tpu as pltpu

F32 = jnp.float32
BF16 = jnp.bfloat16

LN_EPS = 1e-5
RW_GN_EPS = 64e-5
POOL_WINDOWS = (2, 4, 8, 16)
KK_NORM_FLOOR = 1e-12
DECAY_SCALE = math.exp(-0.5)

LANES = 128
SUBLANES = 8
VMEM_LIMIT_BYTES = 56 * 1024 * 1024

CHUNK = 64
HEAD_GROUP = 4


def _dot(a, b):
    return jnp.dot(a.astype(BF16), b.astype(BF16), preferred_element_type=F32)


def _dot_nt(a, b):
    return lax.dot_general(a.astype(BF16), b.astype(BF16), (((1,), (1,)), ((), ())),
                           preferred_element_type=F32)


def _dot_tn(a, b):
    return lax.dot_general(a.astype(BF16), b.astype(BF16), (((0,), (0,)), ((), ())),
                           preferred_element_type=F32)


def _split3(x):
    hi = x.astype(BF16)
    r1 = x - hi.astype(F32)
    mid = r1.astype(BF16)
    lo = (r1 - mid.astype(F32)).astype(BF16)
    return hi, mid, lo


def _layernorm(x, g, b, eps):
    mu = jnp.mean(x, axis=-1, keepdims=True)
    d = x - mu
    var = jnp.mean(d * d, axis=-1, keepdims=True)
    return d * lax.rsqrt(var + eps) * g + b


def _sigmoid(x):
    return 0.5 * jnp.tanh(0.5 * x) + 0.5


def _ada_kernel(c_ref, w_ref, b_ref, o_ref):
    c = c_ref[...]
    cond = c * _sigmoid(c)
    c_hi, c_mid, c_lo = _split3(cond)
    w_hi, w_mid, w_lo = _split3(w_ref[...])
    dd = lambda a, b: jnp.dot(a, b, preferred_element_type=F32)
    acc = dd(c_hi, w_hi) + (dd(c_hi, w_mid) + dd(c_mid, w_hi)) + (dd(c_hi, w_lo) + dd(c_mid, w_mid) + dd(c_lo, w_hi))
    o_ref[...] = acc + b_ref[...]


def _ada_call(c, ada_w, ada_b):
    depth, d, nd = ada_w.shape
    n_ada = nd // d
    bsz = c.shape[0]
    out = pl.pallas_call(
        _ada_kernel,
        grid=(depth, n_ada),
        in_specs=[
            pl.BlockSpec((bsz, d), lambda l, j: (0, 0)),
            pl.BlockSpec((None, d, d), lambda l, j: (l, 0, j)),
            pl.BlockSpec((None, None, 1, d), lambda l, j: (l, j, 0, 0)),
        ],
        out_specs=pl.BlockSpec((None, None, bsz, d), lambda l, j: (l, j, 0, 0)),
        out_shape=jax.ShapeDtypeStruct((depth, n_ada, bsz, d), F32),
        name="ada",
    )(c, ada_w, ada_b.reshape(depth, n_ada, 1, d))
    return jnp.transpose(out, (0, 2, 1, 3))


CONV_ROW_BLOCK = 32
PROJ_COL_CHUNK = 512


def _front_kernel(x_ref, ada_ref, wz_ref, wu_ref, wp_ref, wg_ref, cvw_ref, cvb_ref, lng_ref, lnb_ref, cvwo_ref,
                  plw_ref, pls_ref, plwo_ref, z_ref, g0_ref, m12_ref, hbuf, shbuf, pbuf, acc_s, gate_s,
                  *, tm, cv, conv_width, pg):
    i = pl.program_id(1)
    d = g0_ref.shape[1]
    halo_c = hbuf.shape[0] - tm
    halo_p = pbuf.shape[0] - tm

    @pl.when(i == 0)
    def _():
        hbuf[0:halo_c, :] = jnp.zeros((halo_c, cv), F32)
        pbuf[0:halo_p, :] = jnp.zeros((halo_p, pbuf.shape[1]), F32)

    sh = ada_ref[0:1, :]
    sc = ada_ref[1:2, :]
    h = (x_ref[...] * (1.0 + sc) + sh).astype(BF16)

    def u_job(sl):
        def run():
            u = jnp.dot(h, wu_ref[:, 2 * sl * LANES:2 * (sl + 1) * LANES], preferred_element_type=F32)
            hbuf[halo_c:halo_c + tm, sl * LANES:(sl + 1) * LANES] = u[:, :LANES] * _sigmoid(u[:, LANES:])
        return run

    def p_job():
        pbuf[halo_p:halo_p + tm, :] = jnp.dot(h, wp_ref[...], preferred_element_type=F32)

    def z_job(c0, c1):
        def run():
            z_ref[:, c0:c1] = jnp.dot(h, wz_ref[:, c0:c1], preferred_element_type=F32).astype(z_ref.dtype)
        return run

    def g_job(c0, c1):
        def run():
            gv = jnp.dot(h, wg_ref[:, c0:c1], preferred_element_type=F32)
            if c0 < d:
                g0_ref[:, c0:c1] = gv.astype(g0_ref.dtype)
            else:
                gate_s[:, c0 - d:c1 - d] = _sigmoid(gv)
        return run

    zw = z_ref.shape[1]
    jobs = [p_job]
    jobs += [z_job(c0, min(c0 + PROJ_COL_CHUNK, zw)) for c0 in range(0, zw, PROJ_COL_CHUNK)]
    jobs += [g_job(c0, c0 + PROJ_COL_CHUNK) for c0 in range(0, wg_ref.shape[1], PROJ_COL_CHUNK)]

    base = halo_c - (conv_width - 1)
    span = shbuf.shape[2]
    n_slab = cv // LANES
    n_jobs = len(jobs)
    for sl in range(n_slab):
        u_job(sl)()
        cs = slice(sl * LANES, (sl + 1) * LANES)
        sb = shbuf.at[sl % 2]
        for r in range(1, SUBLANES):
            sb[r - 1] = hbuf[r:r + span, cs]
        for rb in range(tm // CONV_ROW_BLOCK):
            r0 = rb * CONV_ROW_BLOCK
            acc = None
            for j in range(conv_width):
                q, r = divmod(base + j, SUBLANES)
                lo = q * SUBLANES + r0
                src = hbuf[lo:lo + CONV_ROW_BLOCK, cs] if r == 0 else sb[r - 1, lo:lo + CONV_ROW_BLOCK, :]
                term = src * cvw_ref[j:j + 1, cs]
                acc = term if acc is None else acc + term
            acc_s[r0:r0 + CONV_ROW_BLOCK, cs] = acc + cvb_ref[:, cs]
        while jobs and (n_jobs - len(jobs)) * n_slab < (sl + 1) * n_jobs:
            jobs.pop(0)()
    assert not jobs
    hbuf[0:halo_c, :] = hbuf[tm:tm + halo_c, :]

    hn = _layernorm(acc_s[...], lng_ref[...], lnb_ref[...], LN_EPS)
    hn = hn * _sigmoid(hn)
    y_cv = _dot(hn, cvwo_ref[...])

    t1 = (lax.broadcasted_iota(jnp.int32, (tm, 1), 0) + (i * tm + 1)).astype(F32)
    mixed = []
    for gi, win in enumerate(POOL_WINDOWS):
        sl = slice(gi * pg, (gi + 1) * pg)
        ssum = pbuf[halo_p:halo_p + tm, sl]
        for dlt in range(1, win):
            ssum = ssum + pbuf[halo_p - dlt:halo_p - dlt + tm, sl]
        pooled = ssum / jnp.minimum(t1, float(win)) - pbuf[halo_p:halo_p + tm, sl]
        mixed.append(_dot(pooled, plw_ref[gi]))
    mixed = jnp.concatenate(mixed, axis=1) * pls_ref[...]
    y_pl = _dot(mixed, plwo_ref[...])
    pbuf[0:halo_p, :] = pbuf[tm:tm + halo_p, :]

    m12_ref[...] = gate_s[:, 0:d] * y_cv + gate_s[:, d:2 * d] * y_pl


def _front_call(x, ada, wz, wu, wp, wg, cvw, cvb, lng, lnb, cvwo, plw, pls, plwo, tm):
    bsz, s, d = x.shape
    conv_width, cv = cvw.shape
    pw = wp.shape[1]
    pg = plw.shape[1]
    zw = wz.shape[1]
    halo_c = -(-(conv_width - 1) // SUBLANES) * SUBLANES
    halo_p = -(-(max(POOL_WINDOWS) - 1) // SUBLANES) * SUBLANES
    c2 = lambda b, i: (0, 0)
    tile = lambda w: pl.BlockSpec((None, tm, w), lambda b, i: (b, i, 0))
    resident = lambda a: pl.BlockSpec(a.shape, c2, pipeline_mode=pl.Buffered(1))
    kern = functools.partial(_front_kernel, tm=tm, cv=cv, conv_width=conv_width, pg=pg)
    return pl.pallas_call(
        kern,
        grid=(bsz, s // tm),
        in_specs=[
            tile(d),
            pl.BlockSpec((None,) + ada.shape[1:], lambda b, i: (b, 0, 0)),
            resident(wz), resident(wu), resident(wp), resident(wg),
            pl.BlockSpec(cvw.shape, c2),
            pl.BlockSpec((1, cv), c2),
            pl.BlockSpec((1, cv), c2),
            pl.BlockSpec((1, cv), c2),
            resident(cvwo),
            pl.BlockSpec(plw.shape, lambda b, i: (0, 0, 0)),
            pl.BlockSpec((1, pw), c2),
            resident(plwo),
        ],
        out_specs=[tile(zw), tile(d), tile(d)],
        out_shape=[jax.ShapeDtypeStruct((bsz, s, zw), BF16), jax.ShapeDtypeStruct((bsz, s, d), BF16),
                   jax.ShapeDtypeStruct((bsz, s, d), F32)],
        scratch_shapes=[pltpu.VMEM((tm + halo_c, cv), F32),
                        pltpu.VMEM((2, SUBLANES - 1, tm + halo_c - SUBLANES, LANES), F32),
                        pltpu.VMEM((tm + halo_p, pw), F32),
                        pltpu.VMEM((tm, cv), F32),
                        pltpu.VMEM((tm, 2 * d), F32)],
        compiler_params=pltpu.CompilerParams(
            dimension_semantics=("parallel", "arbitrary"), vmem_limit_bytes=VMEM_LIMIT_BYTES),
        name="front",
    )(x, ada, wz, wu, wp, wg, cvw, cvb.reshape(1, cv), lng.reshape(1, cv), lnb.reshape(1, cv), cvwo,
      plw, pls.reshape(1, pw), plwo)


def _head_sum(x, head):
    per_slab = LANES // head
    lane = lax.broadcasted_iota(jnp.int32, (1, LANES), 1)
    out = []
    for s0 in range(0, x.shape[1], LANES):
        slab = x[:, s0:s0 + LANES]
        res = jnp.zeros_like(slab)
        for q in range(per_slab):
            m = (lane >= q * head) & (lane < (q + 1) * head)
            tot = jnp.sum(jnp.where(m, slab, 0.0), axis=-1, keepdims=True)
            res = jnp.where(m, tot, res)
        out.append(res)
    return jnp.concatenate(out, axis=1)


def _rwkv_kernel(*refs, tm, rw, head, has_vres):
    if has_vres:
        (z_ref, zn_ref, vf_ref, vfn_ref, mu_ref, wa2_ref, g2_ref, v2_ref, vec_ref, yg_ref,
         zbuf, r_s, lw_s, k_s, v_s, a_s, b_s, g_s, bonus_s, y_s, h_s) = refs
        vfo_ref = None
    else:
        (z_ref, zn_ref, mu_ref, wa2_ref, g2_ref, vec_ref, yg_ref, vfo_ref,
         zbuf, r_s, lw_s, k_s, v_s, a_s, b_s, g_s, bonus_s, y_s, h_s) = refs
        vf_ref = vfn_ref = v2_ref = None
    i = pl.program_id(1)
    cur = i % 2
    nxt = 1 - cur
    gw = HEAD_GROUP * head
    n_grp = rw // gw
    L = CHUNK
    n_chunk = tm // L
    w0, a0, k_k, k_a, r_k, gn_g, gn_b, v0 = (vec_ref[j:j + 1, :] for j in range(8))
    lane = lax.broadcasted_iota(jnp.int32, (1, LANES), 1)

    def prepare_rows(zsrc, vfsrc, slot, p):
        rows = slice(p * L, (p + 1) * L)
        brow = slice(SUBLANES + p * L, SUBLANES + (p + 1) * L)
        prow = slice(SUBLANES - 1 + p * L, SUBLANES - 1 + (p + 1) * L)

        def shifted(c0, c1):
            zr = zsrc[rows, c0:c1].astype(F32)
            zbuf[brow, c0:c1] = zr
            return zr + (zbuf[prow, c0:c1] - zr) * mu_ref[:, c0:c1]

        lwa = shifted(3 * rw, 3 * rw + LANES)
        lora_in = jnp.where(lane < LANES // 2, jnp.tanh(lwa), lwa)
        wl = _dot(lora_in, wa2_ref[...])
        yield
        lw_s[slot, rows, :] = -DECAY_SCALE * _sigmoid(w0 + wl[:, 0:rw])
        yield
        a = _sigmoid(a0 + wl[:, rw:2 * rw])
        yield
        lg = shifted(3 * rw + LANES, 3 * rw + 2 * LANES)
        g_s[slot, rows, :] = _dot(_sigmoid(lg), g2_ref[...])
        yield
        v = shifted(2 * rw, 3 * rw)
        yield
        if has_vres:
            lv = shifted(3 * rw + 2 * LANES, 3 * rw + 3 * LANES)
            v = v + (vfsrc[rows, :] - v) * _sigmoid(v0 + _dot(lv, v2_ref[...]))
            yield
        v_s[slot, rows, :] = v
        k = shifted(rw, 2 * rw)
        yield
        kk = k * k_k
        n2 = _head_sum(kk * kk, head)
        yield
        kkn = kk * jnp.minimum(lax.rsqrt(n2), 1.0 / KK_NORM_FLOOR)
        a_s[slot, rows, :] = -kkn
        yield
        b_s[slot, rows, :] = kkn * a
        kp = k * (1.0 + (a - 1.0) * k_a)
        k_s[slot, rows, :] = kp
        yield
        r = shifted(0, rw)
        r_s[slot, rows, :] = r
        yield
        bsum = _head_sum(r * kp * r_k, head)
        yield
        bonus_s[slot, rows, :] = bsum * v

    @pl.when(i == 0)
    def _():
        h_s[...] = jnp.zeros(h_s.shape, F32)
        zbuf[0:SUBLANES, :] = jnp.zeros((SUBLANES, zbuf.shape[1]), F32)
        for p in range(n_chunk):
            for _ in prepare_rows(z_ref, vf_ref, cur, p):
                pass

    zbuf[0:SUBLANES, :] = z_ref[tm - SUBLANES:tm, :].astype(F32)

    def prepare_next():
        for p in range(n_chunk):
            yield from prepare_rows(zn_ref, vfn_ref, nxt, p)

    prepare_steps = prepare_next()

    def prepare_some(n=1):
        for _ in range(n):
            next(prepare_steps, None)

    ri = lax.broadcasted_iota(jnp.int32, (L, L), 0)
    ci = lax.broadcasted_iota(jnp.int32, (L, L), 1)
    tri = jnp.where(ri >= ci, 1.0, 0.0).astype(BF16)
    crow = lax.broadcasted_iota(jnp.int32, (L, gw), 0)
    ccol = lax.broadcasted_iota(jnp.int32, (L, gw), 1) % L
    cmask_s = crow > ccol
    cmask_i = crow >= ccol
    eye_c = crow == ccol
    eye_f = jnp.where(eye_c, 1.0, 0.0)
    glane = lax.broadcasted_iota(jnp.int32, (1, gw), 1)
    head_masks = [(glane >= h * head) & (glane < (h + 1) * head) for h in range(HEAD_GROUP)]
    mid = L // 2 - 1
    n_dbl = int(math.log2(L))

    def stack(x):
        xb = x.astype(BF16)
        zero = jnp.zeros_like(xb)
        return jnp.concatenate([jnp.where(m, xb, zero) for m in head_masks], axis=0)

    decay = []
    for c in range(n_chunk):
        rows = slice(c * L, (c + 1) * L)
        lw_c = lw_s[cur, rows, :]
        hi, md, _ = _split3(lw_c)
        dd = lambda a_, b_: jnp.dot(a_, b_, preferred_element_type=F32)
        cl = dd(tri, hi) + dd(tri, md)
        c_mid = cl[mid:mid + 1, :]
        c_end = cl[L - 1:L, :]
        decay.append(dict(w_inc=jnp.exp(cl - c_mid), w_exc=jnp.exp(cl - lw_c - c_mid),
                          w_inv=jnp.exp(c_mid - cl), w_end=jnp.exp(c_end - cl),
                          e0=jnp.exp(c_mid), wl_end=jnp.exp(c_end)))

    inst = [dict(c=c, gi=gi, rows=slice(c * L, (c + 1) * L), ls=slice(gi * gw, (gi + 1) * gw))
            for c in range(n_chunk) for gi in range(n_grp)]

    for st in inst:
        rows, ls, dc = st["rows"], st["ls"], decay[st["c"]]
        st["e0"] = dc["e0"][:, ls]
        st["wl_end"] = dc["wl_end"][:, ls]
        a_t = a_s[cur, rows, ls] * dc["w_exc"][:, ls]
        st["r_t"] = r_s[cur, rows, ls] * dc["w_inc"][:, ls]
        b_t = b_s[cur, rows, ls] * dc["w_inv"][:, ls]
        k_t = k_s[cur, rows, ls] * dc["w_inv"][:, ls]
        b_h = b_s[cur, rows, ls] * dc["w_end"][:, ls]
        k_h = k_s[cur, rows, ls] * dc["w_end"][:, ls]
        st["sv"] = stack(v_s[cur, rows, ls])
        st["sa"] = stack(a_t)
        st["sbk_h"] = jnp.concatenate([stack(b_h), stack(k_h)], axis=0)
        m4 = _dot_nt(jnp.concatenate([a_t, st["r_t"]], axis=0),
                     jnp.concatenate([stack(b_t), stack(k_t)], axis=0))
        st["c_ab"] = jnp.where(cmask_s, m4[0:L, 0:gw], 0.0)
        st["c_ak"] = jnp.where(cmask_s, m4[0:L, gw:2 * gw], 0.0)
        st["c_rb"] = jnp.where(cmask_i, m4[L:2 * L, 0:gw], 0.0)
        st["c_rk"] = jnp.where(cmask_i, m4[L:2 * L, gw:2 * gw], 0.0)
        st["ct"] = _dot_nt(eye_f, st["sbk_h"])
        prepare_some()

    for st in inst:
        st["s_pow"] = _dot(st["c_ab"], stack(st["c_ab"]))
        st["t_c"] = jnp.where(eye_c, 1.0, st["c_ab"])
        xv = _dot(jnp.concatenate([st["c_ak"], st["c_rk"], st["ct"][:, gw:2 * gw]], axis=0), st["sv"])
        st["v2"] = xv[0:L]
        st["rk_v"] = xv[L:2 * L]
        st["kh_v"] = xv[2 * L:3 * L]
        prepare_some()

    for it in range(1, n_dbl):
        for st in inst:
            bd = stack(st["s_pow"])
            if it < n_dbl - 1:
                prod = _dot(jnp.concatenate([st["t_c"], st["s_pow"]], axis=0), bd)
                st["t_c"] = st["t_c"] + prod[0:L]
                st["s_pow"] = prod[L:2 * L]
            else:
                st["t_c"] = st["t_c"] + _dot(st["t_c"], bd)
            prepare_some()

    for st in inst:
        x2 = _dot(st["t_c"], jnp.concatenate([st["sa"], stack(st["v2"])], axis=1))
        st["s_x2"] = jnp.concatenate([stack(x2[:, 0:gw]), stack(x2[:, gw:2 * gw])], axis=1)
        prepare_some()
    for _ in prepare_steps:
        pass

    for st in inst:
        wp = _dot(jnp.concatenate([st["c_rb"], st["ct"][:, 0:gw]], axis=0), st["s_x2"])
        w1, pq = wp[0:L], wp[L:2 * L]
        rhat = (st["r_t"] + w1[:, 0:gw]) * st["e0"]
        p_c = jnp.where(eye_c, st["wl_end"], 0.0) + pq[:, 0:gw] * st["e0"]
        st["rp"] = jnp.concatenate([rhat, p_c], axis=0)
        st["y0"] = w1[:, gw:2 * gw] + st["rk_v"]
        st["q_c"] = pq[:, gw:2 * gw] + st["kh_v"]

    for st in inst:
        gi = st["gi"]
        yh = _dot(st["rp"], stack(h_s[gi]))
        y_s[st["rows"], st["ls"]] = yh[0:L] + st["y0"]
        h_s[gi] = yh[L:2 * L] + st["q_c"]

    y = y_s[...]
    mu = _head_sum(y, head) * (1.0 / head)
    dlt = y - mu
    var = _head_sum(dlt * dlt, head) * (1.0 / head)
    yn = dlt * lax.rsqrt(var + RW_GN_EPS) * gn_g + gn_b
    yg_ref[...] = ((yn + bonus_s[cur]) * g_s[cur]).astype(yg_ref.dtype)
    if not has_vres:
        vfo_ref[...] = v_s[cur]


def _rwkv_call(z, vfirst, mu, wa2, g2, v2, vecs, tm, head):
    bsz, s, zw = z.shape
    rw = g2.shape[1]
    n_tiles = s // tm
    has_vres = vfirst is not None
    c2 = lambda b, i: (0, 0)
    tile = lambda w: pl.BlockSpec((None, tm, w), lambda b, i: (b, i, 0))
    next_tile = lambda w: pl.BlockSpec((None, tm, w), lambda b, i: (b, jnp.minimum(i + 1, n_tiles - 1), 0))
    in_specs = [tile(zw), next_tile(zw)]
    args = [z, z]
    if has_vres:
        in_specs += [tile(rw), next_tile(rw)]
        args += [vfirst, vfirst]
    in_specs += [pl.BlockSpec(mu.shape, c2), pl.BlockSpec(wa2.shape, c2), pl.BlockSpec(g2.shape, c2)]
    args += [mu, wa2, g2]
    if has_vres:
        in_specs.append(pl.BlockSpec(v2.shape, c2))
        args.append(v2)
    in_specs.append(pl.BlockSpec(vecs.shape, c2))
    args.append(vecs)
    out_specs = [tile(rw)]
    out_shape = [jax.ShapeDtypeStruct((bsz, s, rw), BF16)]
    if not has_vres:
        out_specs.append(tile(rw))
        out_shape.append(jax.ShapeDtypeStruct((bsz, s, rw), F32))
    gw = HEAD_GROUP * head
    scratch = [pltpu.VMEM((tm + SUBLANES, zw), F32)]
    scratch += [pltpu.VMEM((2, tm, rw), F32) for _ in range(8)]
    scratch += [pltpu.VMEM((tm, rw), F32)]
    scratch += [pltpu.VMEM((rw // gw, CHUNK, gw), F32)]
    kern = functools.partial(_rwkv_kernel, tm=tm, rw=rw, head=head, has_vres=has_vres)
    res = pl.pallas_call(
        kern,
        grid=(bsz, n_tiles),
        in_specs=in_specs,
        out_specs=out_specs,
        out_shape=out_shape,
        scratch_shapes=scratch,
        compiler_params=pltpu.CompilerParams(
            dimension_semantics=("parallel", "arbitrary"), vmem_limit_bytes=VMEM_LIMIT_BYTES),
        name="rwkv_vres" if has_vres else "rwkv",
    )(*args)
    return (res[0], vfirst) if has_vres else (res[0], res[1])


def _back_kernel(yg_ref, g0_ref, m12_ref, x_ref, ada_ref, wo_ref, wout_ref, lnm_ref, w1_ref, w2_ref, lnf_ref,
                 o_ref, *, alpha, n_split):
    y_rw = jnp.dot(yg_ref[...], wo_ref[...], preferred_element_type=F32)
    merged = _sigmoid(g0_ref[...].astype(F32)) * y_rw + m12_ref[...]
    o = _dot(merged, wout_ref[...])
    x = _layernorm(alpha * x_ref[...] + ada_ref[2:3, :] * o, lnm_ref[0:1, :], lnm_ref[1:2, :], LN_EPS)

    sh = ada_ref[3:4, :]
    sc = ada_ref[4:5, :]
    gt = ada_ref[5:6, :]
    h = (x * (1.0 + sc) + sh).astype(BF16)
    dff = w1_ref.shape[1]
    step = dff // n_split
    acc = None
    for j in range(n_split):
        t = jnp.dot(h, w1_ref[:, j * step:(j + 1) * step], preferred_element_type=F32)
        t = jnp.square(jnp.maximum(t, 0.0)).astype(BF16)
        part = jnp.dot(t, w2_ref[j * step:(j + 1) * step, :], preferred_element_type=F32)
        acc = part if acc is None else acc + part
    o_ref[...] = _layernorm(alpha * x + gt * acc, lnf_ref[0:1, :], lnf_ref[1:2, :], LN_EPS)


def _back_call(yg, g0, m12, x, ada, wo, wout, lnm, w1, w2, lnf, tm, alpha):
    bsz, s, d = x.shape
    dff = w1.shape[1]
    c2 = lambda b, i: (0, 0)
    tile = lambda w: pl.BlockSpec((None, tm, w), lambda b, i: (b, i, 0))
    resident = lambda a: pl.BlockSpec(a.shape, c2, pipeline_mode=pl.Buffered(1))
    kern = functools.partial(_back_kernel, alpha=alpha, n_split=dff // d)
    return pl.pallas_call(
        kern,
        grid=(bsz, s // tm),
        in_specs=[
            tile(yg.shape[-1]), tile(d), tile(d), tile(d),
            pl.BlockSpec((None,) + ada.shape[1:], lambda b, i: (b, 0, 0)),
            resident(wo), resident(wout), pl.BlockSpec(lnm.shape, c2),
            resident(w1), resident(w2), pl.BlockSpec(lnf.shape, c2),
        ],
        out_specs=tile(d),
        out_shape=jax.ShapeDtypeStruct((bsz, s, d), F32),
        compiler_params=pltpu.CompilerParams(
            dimension_semantics=("parallel", "arbitrary"), vmem_limit_bytes=VMEM_LIMIT_BYTES),
        name="back",
    )(yg, g0, m12, x, ada, wo, wout, lnm, w1, w2, lnf)


def _pick_tile(s, pref):
    tm = min(pref, s)
    assert s % tm == 0 and tm % CHUNK == 0, (s, tm)
    return tm


def kernel(x, c, ada_w, ada_b, w_in, w_in_vres, shift_mu, shift_mu_vres, rw_w0, rw_w2, rw_a0, rw_a2, rw_g2, rw_v0, rw_v2, rw_kk, rw_ka, rw_rk, rw_gn_g, rw_gn_b, rw_wo, cv_w, cv_b, cv_ln_g, cv_ln_b, cv_wo, pl_w, pl_scale, pl_wo, w_out, ln_m_g, ln_m_b, mlp_w1, mlp_w2, ln_f_g, ln_f_b):
    bsz, s, d = x.shape
    depth = ada_w.shape[0]
    heads, head = rw_rk.shape[1], rw_rk.shape[2]
    rw = heads * head
    lw_w, la_w, lg_w = rw_w2.shape[1], rw_a2.shape[1], rw_g2.shape[1]
    lv_w = rw_v2.shape[1]
    cv = cv_w.shape[2]
    pw = pl_scale.shape[1]
    assert lw_w + la_w == LANES and lg_w == LANES and lv_w <= LANES and LANES % head == 0
    assert rw % (HEAD_GROUP * head) == 0 and head == CHUNK
    alpha = float((2 * depth) ** 0.25)
    rw_shift = 3 * rw + lw_w + la_w + lg_w
    cv_off, pl_off = rw_shift, rw_shift + 2 * cv
    gt_off = pl_off + pw

    tm_in = _pick_tile(s, 512)
    tm_rw = _pick_tile(s, 512)
    tm_ff = _pick_tile(s, 512)

    ada = _ada_call(c, ada_w, ada_b)

    v_first = None
    for l in range(depth):
        has_vres = l > 0
        w_l = w_in[l]
        wz = w_l[:, :rw_shift]
        mu = shift_mu[l]
        if has_vres:
            pad = LANES - lv_w
            wz = jnp.concatenate([wz, w_in_vres[l - 1], jnp.zeros((d, pad), F32)], axis=1)
            mu = jnp.concatenate([mu, shift_mu_vres[l - 1], jnp.zeros((pad,), F32)])
        wu = w_l[:, cv_off:pl_off].reshape(d, 2, cv // LANES, LANES).transpose(0, 2, 1, 3).reshape(d, 2 * cv)
        wz, wu, wp, wg = (t.astype(BF16) for t in (wz, wu, w_l[:, pl_off:gt_off], w_l[:, gt_off:]))
        z, g, m12 = _front_call(x, ada[l], wz, wu, wp, wg, cv_w[l], cv_b[l], cv_ln_g[l], cv_ln_b[l],
                                cv_wo[l].astype(BF16), pl_w[l].astype(BF16), pl_scale[l], pl_wo[l].astype(BF16),
                                tm_in)

        wa2 = jnp.zeros((LANES, 2 * rw), F32)
        wa2 = wa2.at[:lw_w, :rw].set(rw_w2[l]).at[lw_w:, rw:].set(rw_a2[l]).astype(BF16)
        v0 = rw_v0[l - 1] if has_vres else jnp.zeros((rw,), F32)
        vecs = jnp.stack([rw_w0[l], rw_a0[l], rw_kk[l], rw_ka[l], rw_rk[l].reshape(rw),
                          rw_gn_g[l], rw_gn_b[l], v0])
        v2 = None
        if has_vres:
            v2 = jnp.zeros((LANES, rw), F32).at[:lv_w].set(rw_v2[l - 1]).astype(BF16)
        yg, v_first = _rwkv_call(z, v_first, mu.reshape(1, -1), wa2, rw_g2[l].astype(BF16), v2, vecs, tm_rw, head)

        lnm = jnp.stack([ln_m_g[l], ln_m_b[l]])
        lnf = jnp.stack([ln_f_g[l], ln_f_b[l]])
        x = _back_call(yg, g, m12, x, ada[l], rw_wo[l].astype(BF16), w_out[l].astype(BF16), lnm,
                       mlp_w1[l].astype(BF16), mlp_w2[l].astype(BF16), lnf, tm_ff, alpha)
    return x
```

```python
import functools
import math

import jax
import jax.numpy as jnp
from jax import lax
from jax.experimental import pallas as pl
from jax.experimental.pallas import tpu as pltpu

F32 = jnp.float32
BF16 = jnp.bfloat16

LN_EPS = 1e-5
RW_GN_EPS = 64e-5
POOL_WINDOWS = (2, 4, 8, 16)
KK_NORM_FLOOR = 1e-12
DECAY_SCALE = math.exp(-0.5)

LANES = 128
SUBLANES = 8
VMEM_LIMIT_BYTES = 56 * 1024 * 1024

CHUNK = 64
HEAD_GROUP = 4


def _dot(a, b):
    return jnp.dot(a.astype(BF16), b.astype(BF16), preferred_element_type=F32)


def _dot_nt(a, b):
    return lax.dot_general(a.astype(BF16), b.astype(BF16), (((1,), (1,)), ((), ())),
                           preferred_element_type=F32)


def _dot_tn(a, b):
    return lax.dot_general(a.astype(BF16), b.astype(BF16), (((0,), (0,)), ((), ())),
                           preferred_element_type=F32)


def _split3(x):
    hi = x.astype(BF16)
    r1 = x - hi.astype(F32)
    mid = r1.astype(BF16)
    lo = (r1 - mid.astype(F32)).astype(BF16)
    return hi, mid, lo


def _layernorm(x, g, b, eps):
    mu = jnp.mean(x, axis=-1, keepdims=True)
    d = x - mu
    var = jnp.mean(d * d, axis=-1, keepdims=True)
    return d * lax.rsqrt(var + eps) * g + b


def _sigmoid(x):
    return 0.5 * jnp.tanh(0.5 * x) + 0.5


def _ada_kernel(c_ref, w_ref, b_ref, o_ref):
    c = c_ref[...]
    cond = c * _sigmoid(c)
    c_hi, c_mid, c_lo = _split3(cond)
    w_hi, w_mid, w_lo = _split3(w_ref[...])
    dd = lambda a, b: jnp.dot(a, b, preferred_element_type=F32)
    acc = dd(c_hi, w_hi) + (dd(c_hi, w_mid) + dd(c_mid, w_hi)) + (dd(c_hi, w_lo) + dd(c_mid, w_mid) + dd(c_lo, w_hi))
    o_ref[...] = acc + b_ref[...]


def _ada_call(c, ada_w, ada_b):
    depth, d, nd = ada_w.shape
    n_ada = nd // d
    bsz = c.shape[0]
    out = pl.pallas_call(
        _ada_kernel,
        grid=(depth, n_ada),
        in_specs=[
            pl.BlockSpec((bsz, d), lambda l, j: (0, 0)),
            pl.BlockSpec((None, d, d), lambda l, j: (l, 0, j)),
            pl.BlockSpec((None, None, 1, d), lambda l, j: (l, j, 0, 0)),
        ],
        out_specs=pl.BlockSpec((None, None, bsz, d), lambda l, j: (l, j, 0, 0)),
        out_shape=jax.ShapeDtypeStruct((depth, n_ada, bsz, d), F32),
        name="ada",
    )(c, ada_w, ada_b.reshape(depth, n_ada, 1, d))
    return jnp.transpose(out, (0, 2, 1, 3))


CONV_ROW_BLOCK = 32
PROJ_COL_CHUNK = 512


def _front_kernel(x_ref, ada_ref, wz_ref, wu_ref, wp_ref, wg_ref, cvw_ref, cvb_ref, lng_ref, lnb_ref, cvwo_ref,
                  plw_ref, pls_ref, plwo_ref, z_ref, g0_ref, m12_ref, hbuf, shbuf, pbuf, acc_s, gate_s,
                  *, tm, cv, conv_width, pg):
    i = pl.program_id(1)
    d = g0_ref.shape[1]
    halo_c = hbuf.shape[0] - tm
    halo_p = pbuf.shape[0] - tm

    @pl.when(i == 0)
    def _():
        hbuf[0:halo_c, :] = jnp.zeros((halo_c, cv), F32)
        pbuf[0:halo_p, :] = jnp.zeros((halo_p, pbuf.shape[1]), F32)

    sh = ada_ref[0:1, :]
    sc = ada_ref[1:2, :]
    h = (x_ref[...] * (1.0 + sc) + sh).astype(BF16)

    def u_job(sl):
        def run():
            u = jnp.dot(h, wu_ref[:, 2 * sl * LANES:2 * (sl + 1) * LANES], preferred_element_type=F32)
            hbuf[halo_c:halo_c + tm, sl * LANES:(sl + 1) * LANES] = u[:, :LANES] * _sigmoid(u[:, LANES:])
        return run

    def p_job():
        pbuf[halo_p:halo_p + tm, :] = jnp.dot(h, wp_ref[...], preferred_element_type=F32)

    def z_job(c0, c1):
        def run():
            z_ref[:, c0:c1] = jnp.dot(h, wz_ref[:, c0:c1], preferred_element_type=F32).astype(z_ref.dtype)
        return run

    def g_job(c0, c1):
        def run():
            gv = jnp.dot(h, wg_ref[:, c0:c1], preferred_element_type=F32)
            if c0 < d:
                g0_ref[:, c0:c1] = gv.astype(g0_ref.dtype)
            else:
                gate_s[:, c0 - d:c1 - d] = _sigmoid(gv)
        return run

    zw = z_ref.shape[1]
    jobs = [p_job]
    jobs += [z_job(c0, min(c0 + PROJ_COL_CHUNK, zw)) for c0 in range(0, zw, PROJ_COL_CHUNK)]
    jobs += [g_job(c0, c0 + PROJ_COL_CHUNK) for c0 in range(0, wg_ref.shape[1], PROJ_COL_CHUNK)]

    base = halo_c - (conv_width - 1)
    span = shbuf.shape[2]
    n_slab = cv // LANES
    n_jobs = len(jobs)
    for sl in range(n_slab):
        u_job(sl)()
        cs = slice(sl * LANES, (sl + 1) * LANES)
        sb = shbuf.at[sl % 2]
        for r in range(1, SUBLANES):
            sb[r - 1] = hbuf[r:r + span, cs]
        for rb in range(tm // CONV_ROW_BLOCK):
            r0 = rb * CONV_ROW_BLOCK
            acc = None
            for j in range(conv_width):
                q, r = divmod(base + j, SUBLANES)
                lo = q * SUBLANES + r0
                src = hbuf[lo:lo + CONV_ROW_BLOCK, cs] if r == 0 else sb[r - 1, lo:lo + CONV_ROW_BLOCK, :]
                term = src * cvw_ref[j:j + 1, cs]
                acc = term if acc is None else acc + term
            acc_s[r0:r0 + CONV_ROW_BLOCK, cs] = acc + cvb_ref[:, cs]
        while jobs and (n_jobs - len(jobs)) * n_slab < (sl + 1) * n_jobs:
            jobs.pop(0)()
    assert not jobs
    hbuf[0:halo_c, :] = hbuf[tm:tm + halo_c, :]

    hn = _layernorm(acc_s[...], lng_ref[...], lnb_ref[...], LN_EPS)
    hn = hn * _sigmoid(hn)
    y_cv = _dot(hn, cvwo_ref[...])

    t1 = (lax.broadcasted_iota(jnp.int32, (tm, 1), 0) + (i * tm + 1)).astype(F32)
    mixed = []
    for gi, win in enumerate(POOL_WINDOWS):
        sl = slice(gi * pg, (gi + 1) * pg)
        ssum = pbuf[halo_p:halo_p + tm, sl]
        for dlt in range(1, win):
            ssum = ssum + pbuf[halo_p - dlt:halo_p - dlt + tm, sl]
        pooled = ssum / jnp.minimum(t1, float(win)) - pbuf[halo_p:halo_p + tm, sl]
        mixed.append(_dot(pooled, plw_ref[gi]))
    mixed = jnp.concatenate(mixed, axis=1) * pls_ref[...]
    y_pl = _dot(mixed, plwo_ref[...])
    pbuf[0:halo_p, :] = pbuf[tm:tm + halo_p, :]

    m12_ref[...] = gate_s[:, 0:d] * y_cv + gate_s[:, d:2 * d] * y_pl


def _front_call(x, ada, wz, wu, wp, wg, cvw, cvb, lng, lnb, cvwo, plw, pls, plwo, tm):
    bsz, s, d = x.shape
    conv_width, cv = cvw.shape
    pw = wp.shape[1]
    pg = plw.shape[1]
    zw = wz.shape[1]
    halo_c = -(-(conv_width - 1) // SUBLANES) * SUBLANES
    halo_p = -(-(max(POOL_WINDOWS) - 1) // SUBLANES) * SUBLANES
    c2 = lambda b, i: (0, 0)
    tile = lambda w: pl.BlockSpec((None, tm, w), lambda b, i: (b, i, 0))
    resident = lambda a: pl.BlockSpec(a.shape, c2, pipeline_mode=pl.Buffered(1))
    kern = functools.partial(_front_kernel, tm=tm, cv=cv, conv_width=conv_width, pg=pg)
    return pl.pallas_call(
        kern,
        grid=(bsz, s // tm),
        in_specs=[
            tile(d),
            pl.BlockSpec((None,) + ada.shape[1:], lambda b, i: (b, 0, 0)),
            resident(wz), resident(wu), resident(wp), resident(wg),
            pl.BlockSpec(cvw.shape, c2),
            pl.BlockSpec((1, cv), c2),
            pl.BlockSpec((1, cv), c2),
            pl.BlockSpec((1, cv), c2),
            resident(cvwo),
            pl.BlockSpec(plw.shape, lambda b, i: (0, 0, 0)),
            pl.BlockSpec((1, pw), c2),
            resident(plwo),
        ],
        out_specs=[tile(zw), tile(d), tile(d)],
        out_shape=[jax.ShapeDtypeStruct((bsz, s, zw), F32), jax.ShapeDtypeStruct((bsz, s, d), BF16),
                   jax.ShapeDtypeStruct((bsz, s, d), F32)],
        scratch_shapes=[pltpu.VMEM((tm + halo_c, cv), F32),
                        pltpu.VMEM((2, SUBLANES - 1, tm + halo_c - SUBLANES, LANES), F32),
                        pltpu.VMEM((tm + halo_p, pw), F32),
                        pltpu.VMEM((tm, cv), F32),
                        pltpu.VMEM((tm, 2 * d), F32)],
        compiler_params=pltpu.CompilerParams(
            dimension_semantics=("parallel", "arbitrary"), vmem_limit_bytes=VMEM_LIMIT_BYTES),
        name="front",
    )(x, ada, wz, wu, wp, wg, cvw, cvb.reshape(1, cv), lng.reshape(1, cv), lnb.reshape(1, cv), cvwo,
      plw, pls.reshape(1, pw), plwo)


def _head_sum(x, head):
    per_slab = LANES // head
    lane = lax.broadcasted_iota(jnp.int32, (1, LANES), 1)
    out = []
    for s0 in range(0, x.shape[1], LANES):
        slab = x[:, s0:s0 + LANES]
        res = jnp.zeros_like(slab)
        for q in range(per_slab):
            m = (lane >= q * head) & (lane < (q + 1) * head)
            tot = jnp.sum(jnp.where(m, slab, 0.0), axis=-1, keepdims=True)
            res = jnp.where(m, tot, res)
        out.append(res)
    return jnp.concatenate(out, axis=1)


def _rwkv_kernel(*refs, tm, rw, head, has_vres):
    if has_vres:
        (z_ref, vf_ref, mu_ref, wa2_ref, g2_ref, v2_ref, vec_ref, yg_ref, zbuf, h_s) = refs
        vfo_ref = None
    else:
        (z_ref, mu_ref, wa2_ref, g2_ref, vec_ref, yg_ref, vfo_ref, zbuf, h_s) = refs
        vf_ref = v2_ref = None
    i = pl.program_id(1)
    gw = HEAD_GROUP * head
    n_grp = rw // gw
    L = CHUNK
    n_chunk = tm // L
    w0, a0, k_k, k_a, r_k, gn_g, gn_b, v0 = (vec_ref[j:j + 1, :] for j in range(8))
    lane = lax.broadcasted_iota(jnp.int32, (1, LANES), 1)

    @pl.when(i == 0)
    def _():
        h_s[...] = jnp.zeros(h_s.shape, F32)
        zbuf[0:SUBLANES, :] = jnp.zeros((SUBLANES, zbuf.shape[1]), F32)

    def prepare(ch):
        rows = ch["rows"]
        p = ch["c"]

        def shifted(c0, c1):
            zr = z_ref[rows, c0:c1]
            if p == 0:
                zbuf[SUBLANES:SUBLANES + L, c0:c1] = zr
                prev = zbuf[SUBLANES - 1:SUBLANES - 1 + L, c0:c1]
            else:
                prev = z_ref[p * L - 1:(p + 1) * L - 1, c0:c1]
            return zr + (prev - zr) * mu_ref[:, c0:c1]

        lwa = shifted(3 * rw, 3 * rw + LANES)
        lora_in = jnp.where(lane < LANES // 2, jnp.tanh(lwa), lwa)
        wl = _dot(lora_in, wa2_ref[...])
        lw_c = -DECAY_SCALE * _sigmoid(w0 + wl[:, 0:rw])
        a = _sigmoid(a0 + wl[:, rw:2 * rw])
        lg = shifted(3 * rw + LANES, 3 * rw + 2 * LANES)
        ch["g"] = _dot(_sigmoid(lg), g2_ref[...])
        v = shifted(2 * rw, 3 * rw)
        if has_vres:
            lv = shifted(3 * rw + 2 * LANES, 3 * rw + 3 * LANES)
            v = v + (vf_ref[rows, :] - v) * _sigmoid(v0 + _dot(lv, v2_ref[...]))
        else:
            vfo_ref[rows, :] = v
        k = shifted(rw, 2 * rw)
        kk = k * k_k
        kkn = kk * jnp.minimum(lax.rsqrt(_head_sum(kk * kk, head)), 1.0 / KK_NORM_FLOOR)
        kp = k * (1.0 + (a - 1.0) * k_a)
        r = shifted(0, rw)
        ch["bonus"] = _head_sum(r * kp * r_k, head) * v
        hi, md, _ = _split3(lw_c)
        dd = lambda a_, b_: jnp.dot(a_, b_, preferred_element_type=F32)
        cl = dd(tri, hi) + dd(tri, md)
        c_mid = cl[mid:mid + 1, :]
        c_end = cl[L - 1:L, :]
        w_inv = jnp.exp(c_mid - cl)
        w_end = jnp.exp(c_end - cl)
        ch["e0"] = jnp.exp(c_mid)
        ch["wl_end"] = jnp.exp(c_end)
        ch["a_t"] = -kkn * jnp.exp(cl - lw_c - c_mid)
        ch["r_t"] = r * jnp.exp(cl - c_mid)
        b_v = kkn * a
        ch["b_t"] = b_v * w_inv
        ch["k_t"] = kp * w_inv
        ch["b_h"] = b_v * w_end
        ch["k_h"] = kp * w_end
        ch["v"] = v

    ri = lax.broadcasted_iota(jnp.int32, (L, L), 0)
    ci = lax.broadcasted_iota(jnp.int32, (L, L), 1)
    tri = jnp.where(ri >= ci, 1.0, 0.0).astype(BF16)
    crow = lax.broadcasted_iota(jnp.int32, (L, gw), 0)
    ccol = lax.broadcasted_iota(jnp.int32, (L, gw), 1) % L
    cmask_s = crow > ccol
    cmask_i = crow >= ccol
    eye_c = crow == ccol
    eye_f = jnp.where(eye_c, 1.0, 0.0)
    glane = lax.broadcasted_iota(jnp.int32, (1, gw), 1)
    head_masks = [(glane >= h * head) & (glane < (h + 1) * head) for h in range(HEAD_GROUP)]
    mid = L // 2 - 1
    n_dbl = int(math.log2(L))

    def stack(x):
        xb = x.astype(BF16)
        zero = jnp.zeros_like(xb)
        return jnp.concatenate([jnp.where(m, xb, zero) for m in head_masks], axis=0)

    def pair_products(ch):
        for gi in range(n_grp):
            ls = slice(gi * gw, (gi + 1) * gw)
            st = ch["grp"][gi]
            a_t, st["r_t"] = ch["a_t"][:, ls], ch["r_t"][:, ls]
            st["e0"], st["wl_end"] = ch["e0"][:, ls], ch["wl_end"][:, ls]
            st["sv"] = stack(ch["v"][:, ls])
            st["sa"] = stack(a_t)
            m4 = _dot_nt(jnp.concatenate([a_t, st["r_t"]], axis=0),
                         jnp.concatenate([stack(ch["b_t"][:, ls]), stack(ch["k_t"][:, ls])], axis=0))
            st["c_ab"] = jnp.where(cmask_s, m4[0:L, 0:gw], 0.0)
            st["c_ak"] = jnp.where(cmask_s, m4[0:L, gw:2 * gw], 0.0)
            st["c_rb"] = jnp.where(cmask_i, m4[L:2 * L, 0:gw], 0.0)
            st["c_rk"] = jnp.where(cmask_i, m4[L:2 * L, gw:2 * gw], 0.0)
            st["ct"] = _dot_nt(eye_f, jnp.concatenate([stack(ch["b_h"][:, ls]), stack(ch["k_h"][:, ls])],
                                                      axis=0))

    def square_and_v(ch):
        for st in ch["grp"]:
            st["s_pow"] = _dot(st["c_ab"], stack(st["c_ab"]))
            st["t_c"] = jnp.where(eye_c, 1.0, st["c_ab"])
            xv = _dot(jnp.concatenate([st["c_ak"], st["c_rk"], st["ct"][:, gw:2 * gw]], axis=0), st["sv"])
            st["v2"] = xv[0:L]
            st["rk_v"] = xv[L:2 * L]
            st["kh_v"] = xv[2 * L:3 * L]

    def doubling(it):
        def run(ch):
            for st in ch["grp"]:
                bd = stack(st["s_pow"])
                if it < n_dbl - 1:
                    prod = _dot(jnp.concatenate([st["t_c"], st["s_pow"]], axis=0), bd)
                    st["t_c"] = st["t_c"] + prod[0:L]
                    st["s_pow"] = prod[L:2 * L]
                else:
                    st["t_c"] = st["t_c"] + _dot(st["t_c"], bd)
        return run

    def solve(ch):
        for st in ch["grp"]:
            x2 = _dot(st["t_c"], jnp.concatenate([st["sa"], stack(st["v2"])], axis=1))
            st["s_x2"] = jnp.concatenate([stack(x2[:, 0:gw]), stack(x2[:, gw:2 * gw])], axis=1)

    def transfer(ch):
        for st in ch["grp"]:
            wp = _dot(jnp.concatenate([st["c_rb"], st["ct"][:, 0:gw]], axis=0), st["s_x2"])
            w1, pq = wp[0:L], wp[L:2 * L]
            rhat = (st["r_t"] + w1[:, 0:gw]) * st["e0"]
            p_c = jnp.where(eye_c, st["wl_end"], 0.0) + pq[:, 0:gw] * st["e0"]
            st["rp"] = jnp.concatenate([rhat, p_c], axis=0)
            st["y0"] = w1[:, gw:2 * gw] + st["rk_v"]
            st["q_c"] = pq[:, gw:2 * gw] + st["kh_v"]

    def state_pass(ch):
        ys = []
        for gi, st in enumerate(ch["grp"]):
            yh = _dot(st["rp"], stack(h_s[gi]))
            ys.append(yh[0:L] + st["y0"])
            h_s[gi] = yh[L:2 * L] + st["q_c"]
        ch["y"] = jnp.concatenate(ys, axis=1)

    def finish(ch):
        y = ch["y"]
        mean = _head_sum(y, head) * (1.0 / head)
        dlt = y - mean
        var = _head_sum(dlt * dlt, head) * (1.0 / head)
        yn = dlt * lax.rsqrt(var + RW_GN_EPS) * gn_g + gn_b
        yg_ref[ch["rows"], :] = ((yn + ch["bonus"]) * ch["g"]).astype(yg_ref.dtype)

    stages = [prepare, pair_products, square_and_v] + [doubling(it) for it in range(1, n_dbl)]
    stages += [solve, transfer, state_pass, finish]
    chunks = [dict(c=c, rows=slice(c * L, (c + 1) * L), grp=[dict() for _ in range(n_grp)])
              for c in range(n_chunk)]
    for step in range(n_chunk + len(stages) - 1):
        for ch in chunks:
            if 0 <= step - ch["c"] < len(stages):
                stages[step - ch["c"]](ch)

    zbuf[0:SUBLANES, :] = z_ref[tm - SUBLANES:tm, :]


def _rwkv_call(z, vfirst, mu, wa2, g2, v2, vecs, tm, head):
    bsz, s, zw = z.shape
    rw = g2.shape[1]
    n_tiles = s // tm
    has_vres = vfirst is not None
    c2 = lambda b, i: (0, 0)
    tile = lambda w: pl.BlockSpec((None, tm, w), lambda b, i: (b, i, 0))
    in_specs = [tile(zw)]
    args = [z]
    if has_vres:
        in_specs.append(tile(rw))
        args.append(vfirst)
    in_specs += [pl.BlockSpec(mu.shape, c2), pl.BlockSpec(wa2.shape, c2), pl.BlockSpec(g2.shape, c2)]
    args += [mu, wa2, g2]
    if has_vres:
        in_specs.append(pl.BlockSpec(v2.shape, c2))
        args.append(v2)
    in_specs.append(pl.BlockSpec(vecs.shape, c2))
    args.append(vecs)
    out_specs = [tile(rw)]
    out_shape = [jax.ShapeDtypeStruct((bsz, s, rw), BF16)]
    if not has_vres:
        out_specs.append(tile(rw))
        out_shape.append(jax.ShapeDtypeStruct((bsz, s, rw), F32))
    gw = HEAD_GROUP * head
    scratch = [pltpu.VMEM((CHUNK + SUBLANES, zw), F32),
               pltpu.VMEM((rw // gw, CHUNK, gw), F32)]
    kern = functools.partial(_rwkv_kernel, tm=tm, rw=rw, head=head, has_vres=has_vres)
    res = pl.pallas_call(
        kern,
        grid=(bsz, n_tiles),
        in_specs=in_specs,
        out_specs=out_specs,
        out_shape=out_shape,
        scratch_shapes=scratch,
        compiler_params=pltpu.CompilerParams(
            dimension_semantics=("parallel", "arbitrary"), vmem_limit_bytes=VMEM_LIMIT_BYTES),
        name="rwkv_vres" if has_vres else "rwkv",
    )(*args)
    return (res[0], vfirst) if has_vres else (res[0], res[1])


def _back_kernel(yg_ref, g0_ref, m12_ref, x_ref, ada_ref, wo_ref, wout_ref, lnm_ref, w1_ref, w2_ref, lnf_ref,
                 o_ref, *, alpha, n_split):
    y_rw = jnp.dot(yg_ref[...], wo_ref[...], preferred_element_type=F32)
    merged = _sigmoid(g0_ref[...].astype(F32)) * y_rw + m12_ref[...]
    o = _dot(merged, wout_ref[...])
    x = _layernorm(alpha * x_ref[...] + ada_ref[2:3, :] * o, lnm_ref[0:1, :], lnm_ref[1:2, :], LN_EPS)

    sh = ada_ref[3:4, :]
    sc = ada_ref[4:5, :]
    gt = ada_ref[5:6, :]
    h = (x * (1.0 + sc) + sh).astype(BF16)
    dff = w1_ref.shape[1]
    step = dff // n_split
    acc = None
    for j in range(n_split):
        t = jnp.dot(h, w1_ref[:, j * step:(j + 1) * step], preferred_element_type=F32)
        t = jnp.square(jnp.maximum(t, 0.0)).astype(BF16)
        part = jnp.dot(t, w2_ref[j * step:(j + 1) * step, :], preferred_element_type=F32)
        acc = part if acc is None else acc + part
    o_ref[...] = _layernorm(alpha * x + gt * acc, lnf_ref[0:1, :], lnf_ref[1:2, :], LN_EPS)


def _back_call(yg, g0, m12, x, ada, wo, wout, lnm, w1, w2, lnf, tm, alpha):
    bsz, s, d = x.shape
    dff = w1.shape[1]
    c2 = lambda b, i: (0, 0)
    tile = lambda w: pl.BlockSpec((None, tm, w), lambda b, i: (b, i, 0))
    resident = lambda a: pl.BlockSpec(a.shape, c2, pipeline_mode=pl.Buffered(1))
    kern = functools.partial(_back_kernel, alpha=alpha, n_split=dff // d)
    return pl.pallas_call(
        kern,
        grid=(bsz, s // tm),
        in_specs=[
            tile(yg.shape[-1]), tile(d), tile(d), tile(d),
            pl.BlockSpec((None,) + ada.shape[1:], lambda b, i: (b, 0, 0)),
            resident(wo), resident(wout), pl.BlockSpec(lnm.shape, c2),
            resident(w1), resident(w2), pl.BlockSpec(lnf.shape, c2),
        ],
        out_specs=tile(d),
        out_shape=jax.ShapeDtypeStruct((bsz, s, d), F32),
        compiler_params=pltpu.CompilerParams(
            dimension_semantics=("parallel", "arbitrary"), vmem_limit_bytes=VMEM_LIMIT_BYTES),
        name="back",
    )(yg, g0, m12, x, ada, wo, wout, lnm, w1, w2, lnf)


def _pick_tile(s, pref):
    tm = min(pref, s)
    assert s % tm == 0 and tm % CHUNK == 0, (s, tm)
    return tm


def kernel(x, c, ada_w, ada_b, w_in, w_in_vres, shift_mu, shift_mu_vres, rw_w0, rw_w2, rw_a0, rw_a2, rw_g2, rw_v0, rw_v2, rw_kk, rw_ka, rw_rk, rw_gn_g, rw_gn_b, rw_wo, cv_w, cv_b, cv_ln_g, cv_ln_b, cv_wo, pl_w, pl_scale, pl_wo, w_out, ln_m_g, ln_m_b, mlp_w1, mlp_w2, ln_f_g, ln_f_b):
    bsz, s, d = x.shape
    depth = ada_w.shape[0]
    heads, head = rw_rk.shape[1], rw_rk.shape[2]
    rw = heads * head
    lw_w, la_w, lg_w = rw_w2.shape[1], rw_a2.shape[1], rw_g2.shape[1]
    lv_w = rw_v2.shape[1]
    cv = cv_w.shape[2]
    pw = pl_scale.shape[1]
    assert lw_w + la_w == LANES and lg_w == LANES and lv_w <= LANES and LANES % head == 0
    assert rw % (HEAD_GROUP * head) == 0 and head == CHUNK
    alpha = float((2 * depth) ** 0.25)
    rw_shift = 3 * rw + lw_w + la_w + lg_w
    cv_off, pl_off = rw_shift, rw_shift + 2 * cv
    gt_off = pl_off + pw

    tm_in = _pick_tile(s, 512)
    tm_rw = _pick_tile(s, 1024)
    tm_ff = _pick_tile(s, 512)

    ada = _ada_call(c, ada_w, ada_b)

    v_first = None
    for l in range(depth):
        has_vres = l > 0
        w_l = w_in[l]
        wz = w_l[:, :rw_shift]
        mu = shift_mu[l]
        if has_vres:
            pad = LANES - lv_w
            wz = jnp.concatenate([wz, w_in_vres[l - 1], jnp.zeros((d, pad), F32)], axis=1)
            mu = jnp.concatenate([mu, shift_mu_vres[l - 1], jnp.zeros((pad,), F32)])
        wu = w_l[:, cv_off:pl_off].reshape(d, 2, cv // LANES, LANES).transpose(0, 2, 1, 3).reshape(d, 2 * cv)
        wz, wu, wp, wg = (t.astype(BF16) for t in (wz, wu, w_l[:, pl_off:gt_off], w_l[:, gt_off:]))
        z, g, m12 = _front_call(x, ada[l], wz, wu, wp, wg, cv_w[l], cv_b[l], cv_ln_g[l], cv_ln_b[l],
                                cv_wo[l].astype(BF16), pl_w[l].astype(BF16), pl_scale[l], pl_wo[l].astype(BF16),
                                tm_in)

        wa2 = jnp.zeros((LANES, 2 * rw), F32)
        wa2 = wa2.at[:lw_w, :rw].set(rw_w2[l]).at[lw_w:, rw:].set(rw_a2[l]).astype(BF16)
        v0 = rw_v0[l - 1] if has_vres else jnp.zeros((rw,), F32)
        vecs = jnp.stack([rw_w0[l], rw_a0[l], rw_kk[l], rw_ka[l], rw_rk[l].reshape(rw),
                          rw_gn_g[l], rw_gn_b[l], v0])
        v2 = None
        if has_vres:
            v2 = jnp.zeros((LANES, rw), F32).at[:lv_w].set(rw_v2[l - 1]).astype(BF16)
        yg, v_first = _rwkv_call(z, v_first, mu.reshape(1, -1), wa2, rw_g2[l].astype(BF16), v2, vecs, tm_rw, head)

        lnm = jnp.stack([ln_m_g[l], ln_m_b[l]])
        lnf = jnp.stack([ln_f_g[l], ln_f_b[l]])
        x = _back_call(yg, g, m12, x, ada[l], rw_wo[l].astype(BF16), w_out[l].astype(BF16), lnm,
                       mlp_w1[l].astype(BF16), mlp_w2[l].astype(BF16), lnf, tm_ff, alpha)
    return x
```

```python
import functools
import math

import jax
import jax.numpy as jnp
from jax import lax
from jax.experimental import pallas as pl
from jax.experimental.pallas import tpu as pltpu

F32 = jnp.float32
BF16 = jnp.bfloat16

LN_EPS = 1e-5
RW_GN_EPS = 64e-5
POOL_WINDOWS = (2, 4, 8, 16)
KK_NORM_FLOOR = 1e-12
DECAY_SCALE = math.exp(-0.5)

LANES = 128
SUBLANES = 8
VMEM_LIMIT_BYTES = 56 * 1024 * 1024

CHUNK = 64
HEAD_GROUP = 4


def _dot(a, b):
    return jnp.dot(a.astype(BF16), b.astype(BF16), preferred_element_type=F32)


def _dot_nt(a, b):
    return lax.dot_general(a.astype(BF16), b.astype(BF16), (((1,), (1,)), ((), ())),
                           preferred_element_type=F32)


def _dot_tn(a, b):
    return lax.dot_general(a.astype(BF16), b.astype(BF16), (((0,), (0,)), ((), ())),
                           preferred_element_type=F32)


def _split3(x):
    hi = x.astype(BF16)
    r1 = x - hi.astype(F32)
    mid = r1.astype(BF16)
    lo = (r1 - mid.astype(F32)).astype(BF16)
    return hi, mid, lo


def _layernorm(x, g, b, eps):
    mu = jnp.mean(x, axis=-1, keepdims=True)
    d = x - mu
    var = jnp.mean(d * d, axis=-1, keepdims=True)
    return d * lax.rsqrt(var + eps) * g + b


def _sigmoid(x):
    return 0.5 * jnp.tanh(0.5 * x) + 0.5


def _ada_kernel(c_ref, w_ref, b_ref, o_ref):
    c = c_ref[...]
    cond = c * _sigmoid(c)
    c_hi, c_mid, c_lo = _split3(cond)
    w_hi, w_mid, w_lo = _split3(w_ref[...])
    dd = lambda a, b: jnp.dot(a, b, preferred_element_type=F32)
    acc = dd(c_hi, w_hi) + (dd(c_hi, w_mid) + dd(c_mid, w_hi)) + (dd(c_hi, w_lo) + dd(c_mid, w_mid) + dd(c_lo, w_hi))
    o_ref[...] = acc + b_ref[...]


def _ada_call(c, ada_w, ada_b):
    depth, d, nd = ada_w.shape
    n_ada = nd // d
    bsz = c.shape[0]
    out = pl.pallas_call(
        _ada_kernel,
        grid=(depth, n_ada),
        in_specs=[
            pl.BlockSpec((bsz, d), lambda l, j: (0, 0)),
            pl.BlockSpec((None, d, d), lambda l, j: (l, 0, j)),
            pl.BlockSpec((None, None, 1, d), lambda l, j: (l, j, 0, 0)),
        ],
        out_specs=pl.BlockSpec((None, None, bsz, d), lambda l, j: (l, j, 0, 0)),
        out_shape=jax.ShapeDtypeStruct((depth, n_ada, bsz, d), F32),
        name="ada",
    )(c, ada_w, ada_b.reshape(depth, n_ada, 1, d))
    return jnp.transpose(out, (0, 2, 1, 3))


CONV_ROW_BLOCK = 32
PROJ_COL_CHUNK = 512


def _front_kernel(x_ref, ada_ref, wz_ref, wu_ref, wp_ref, wg_ref, cvw_ref, cvb_ref, lng_ref, lnb_ref, cvwo_ref,
                  plw_ref, pls_ref, plwo_ref, z_ref, g0_ref, m12_ref, hbuf, shbuf, pbuf, acc_s, gate_s,
                  *, tm, cv, conv_width, pg):
    i = pl.program_id(1)
    d = g0_ref.shape[1]
    halo_c = hbuf.shape[0] - tm
    halo_p = pbuf.shape[0] - tm

    @pl.when(i == 0)
    def _():
        hbuf[0:halo_c, :] = jnp.zeros((halo_c, cv), F32)
        pbuf[0:halo_p, :] = jnp.zeros((halo_p, pbuf.shape[1]), F32)

    sh = ada_ref[0:1, :]
    sc = ada_ref[1:2, :]
    h = (x_ref[...] * (1.0 + sc) + sh).astype(BF16)

    def u_job(sl):
        def run():
            u = jnp.dot(h, wu_ref[:, 2 * sl * LANES:2 * (sl + 1) * LANES], preferred_element_type=F32)
            hbuf[halo_c:halo_c + tm, sl * LANES:(sl + 1) * LANES] = u[:, :LANES] * _sigmoid(u[:, LANES:])
        return run

    def p_job():
        pbuf[halo_p:halo_p + tm, :] = jnp.dot(h, wp_ref[...], preferred_element_type=F32)

    def z_job(c0, c1):
        def run():
            z_ref[:, c0:c1] = jnp.dot(h, wz_ref[:, c0:c1], preferred_element_type=F32).astype(z_ref.dtype)
        return run

    def g_job(c0, c1):
        def run():
            gv = jnp.dot(h, wg_ref[:, c0:c1], preferred_element_type=F32)
            if c0 < d:
                g0_ref[:, c0:c1] = gv.astype(g0_ref.dtype)
            else:
                gate_s[:, c0 - d:c1 - d] = _sigmoid(gv)
        return run

    zw = z_ref.shape[1]
    jobs = [p_job]
    jobs += [z_job(c0, min(c0 + PROJ_COL_CHUNK, zw)) for c0 in range(0, zw, PROJ_COL_CHUNK)]
    jobs += [g_job(c0, c0 + PROJ_COL_CHUNK) for c0 in range(0, wg_ref.shape[1], PROJ_COL_CHUNK)]

    base = halo_c - (conv_width - 1)
    span = shbuf.shape[2]
    n_slab = cv // LANES
    n_jobs = len(jobs)
    for sl in range(n_slab):
        u_job(sl)()
        cs = slice(sl * LANES, (sl + 1) * LANES)
        sb = shbuf.at[sl % 2]
        for r in range(1, SUBLANES):
            sb[r - 1] = hbuf[r:r + span, cs]
        for rb in range(tm // CONV_ROW_BLOCK):
            r0 = rb * CONV_ROW_BLOCK
            acc = None
            for j in range(conv_width):
                q, r = divmod(base + j, SUBLANES)
                lo = q * SUBLANES + r0
                src = hbuf[lo:lo + CONV_ROW_BLOCK, cs] if r == 0 else sb[r - 1, lo:lo + CONV_ROW_BLOCK, :]
                term = src * cvw_ref[j:j + 1, cs]
                acc = term if acc is None else acc + term
            acc_s[r0:r0 + CONV_ROW_BLOCK, cs] = acc + cvb_ref[:, cs]
        while jobs and (n_jobs - len(jobs)) * n_slab < (sl + 1) * n_jobs:
            jobs.pop(0)()
    assert not jobs
    hbuf[0:halo_c, :] = hbuf[tm:tm + halo_c, :]

    hn = _layernorm(acc_s[...], lng_ref[...], lnb_ref[...], LN_EPS)
    hn = hn * _sigmoid(hn)
    y_cv = _dot(hn, cvwo_ref[...])

    t1 = (lax.broadcasted_iota(jnp.int32, (tm, 1), 0) + (i * tm + 1)).astype(F32)
    mixed = []
    for gi, win in enumerate(POOL_WINDOWS):
        ps = slice(gi * pg, (gi + 1) * pg)
        ssum = pbuf[halo_p:halo_p + tm, ps]
        for dlt in range(1, win):
            ssum = ssum + pbuf[halo_p - dlt:halo_p - dlt + tm, ps]
        pooled = ssum / jnp.minimum(t1, float(win)) - pbuf[halo_p:halo_p + tm, ps]
        mixed.append(_dot(pooled, plw_ref[gi]))
    mixed = jnp.concatenate(mixed, axis=1) * pls_ref[...]
    y_pl = _dot(mixed, plwo_ref[...])
    pbuf[0:halo_p, :] = pbuf[tm:tm + halo_p, :]

    m12_ref[...] = gate_s[:, 0:d] * y_cv + gate_s[:, d:2 * d] * y_pl


def _front_call(x, ada, wz, wu, wp, wg, cvw, cvb, lng, lnb, cvwo, plw, pls, plwo, tm):
    bsz, s, d = x.shape
    conv_width, cv = cvw.shape
    pw = wp.shape[1]
    pg = plw.shape[1]
    zw = wz.shape[1]
    halo_c = -(-(conv_width - 1) // SUBLANES) * SUBLANES
    halo_p = -(-(max(POOL_WINDOWS) - 1) // SUBLANES) * SUBLANES
    c2 = lambda b, i: (0, 0)
    tile = lambda w: pl.BlockSpec((None, tm, w), lambda b, i: (b, i, 0))
    resident = lambda a: pl.BlockSpec(a.shape, c2, pipeline_mode=pl.Buffered(1))
    kern = functools.partial(_front_kernel, tm=tm, cv=cv, conv_width=conv_width, pg=pg)
    return pl.pallas_call(
        kern,
        grid=(bsz, s // tm),
        in_specs=[
            tile(d),
            pl.BlockSpec((None,) + ada.shape[1:], lambda b, i: (b, 0, 0)),
            resident(wz), resident(wu), resident(wp), resident(wg),
            pl.BlockSpec(cvw.shape, c2),
            pl.BlockSpec((1, cv), c2),
            pl.BlockSpec((1, cv), c2),
            pl.BlockSpec((1, cv), c2),
            resident(cvwo),
            pl.BlockSpec(plw.shape, lambda b, i: (0, 0, 0)),
            pl.BlockSpec((1, pw), c2),
            resident(plwo),
        ],
        out_specs=[tile(zw), tile(d), tile(d)],
        out_shape=[jax.ShapeDtypeStruct((bsz, s, zw), F32), jax.ShapeDtypeStruct((bsz, s, d), BF16),
                   jax.ShapeDtypeStruct((bsz, s, d), F32)],
        scratch_shapes=[pltpu.VMEM((tm + halo_c, cv), F32),
                        pltpu.VMEM((2, SUBLANES - 1, tm + halo_c - SUBLANES, LANES), F32),
                        pltpu.VMEM((tm + halo_p, pw), F32),
                        pltpu.VMEM((tm, cv), F32),
                        pltpu.VMEM((tm, 2 * d), F32)],
        compiler_params=pltpu.CompilerParams(
            dimension_semantics=("parallel", "arbitrary"), vmem_limit_bytes=VMEM_LIMIT_BYTES),
        name="front",
    )(x, ada, wz, wu, wp, wg, cvw, cvb.reshape(1, cv), lng.reshape(1, cv), lnb.reshape(1, cv), cvwo,
      plw, pls.reshape(1, pw), plwo)


def _head_sum(x, head):
    assert LANES == 2 * head
    low = lax.broadcasted_iota(jnp.int32, (1, LANES), 1) < head
    out = []
    for s0 in range(0, x.shape[1], LANES):
        slab = x[:, s0:s0 + LANES]
        tot_low = jnp.sum(jnp.where(low, slab, 0.0), axis=-1, keepdims=True)
        tot_high = jnp.sum(slab, axis=-1, keepdims=True) - tot_low
        out.append(jnp.where(low, tot_low, tot_high))
    return jnp.concatenate(out, axis=1)


def _rwkv_kernel(*refs, tm, rw, head, has_vres):
    if has_vres:
        (z_ref, vf_ref, mu_ref, wa2_ref, g2_ref, v2_ref, vec_ref, yg_ref, zbuf, h_s) = refs
        vfo_ref = None
    else:
        (z_ref, mu_ref, wa2_ref, g2_ref, vec_ref, yg_ref, vfo_ref, zbuf, h_s) = refs
        vf_ref = v2_ref = None
    i = pl.program_id(1)
    gw = HEAD_GROUP * head
    n_grp = rw // gw
    L = CHUNK
    n_chunk = tm // L
    w0, a0, k_k, k_a, r_k, gn_g, gn_b, v0 = (vec_ref[j:j + 1, :] for j in range(8))
    lane = lax.broadcasted_iota(jnp.int32, (1, LANES), 1)

    @pl.when(i == 0)
    def _():
        h_s[...] = jnp.zeros(h_s.shape, F32)
        zbuf[0:SUBLANES, :] = jnp.zeros((SUBLANES, zbuf.shape[1]), F32)

    def prepare(ch):
        rows = ch["rows"]
        p = ch["c"]

        def shifted(c0, c1):
            zr = z_ref[rows, c0:c1]
            if p == 0:
                zbuf[SUBLANES:SUBLANES + L, c0:c1] = zr
                prev = zbuf[SUBLANES - 1:SUBLANES - 1 + L, c0:c1]
            else:
                prev = z_ref[p * L - 1:(p + 1) * L - 1, c0:c1]
            return zr + (prev - zr) * mu_ref[:, c0:c1]

        lwa = shifted(3 * rw, 3 * rw + LANES)
        lora_in = jnp.where(lane < LANES // 2, jnp.tanh(lwa), lwa)
        wl = _dot(lora_in, wa2_ref[...])
        lw_c = -DECAY_SCALE * _sigmoid(w0 + wl[:, 0:rw])
        a = _sigmoid(a0 + wl[:, rw:2 * rw])
        lg = shifted(3 * rw + LANES, 3 * rw + 2 * LANES)
        ch["g"] = _dot(_sigmoid(lg), g2_ref[...])
        v = shifted(2 * rw, 3 * rw)
        if has_vres:
            lv = shifted(3 * rw + 2 * LANES, 3 * rw + 3 * LANES)
            v = v + (vf_ref[rows, :] - v) * _sigmoid(v0 + _dot(lv, v2_ref[...]))
        else:
            vfo_ref[rows, :] = v
        k = shifted(rw, 2 * rw)
        kk = k * k_k
        kkn = kk * jnp.minimum(lax.rsqrt(_head_sum(kk * kk, head)), 1.0 / KK_NORM_FLOOR)
        kp = k * (1.0 + (a - 1.0) * k_a)
        r = shifted(0, rw)
        ch["bonus"] = _head_sum(r * kp * r_k, head) * v
        hi, md, _ = _split3(lw_c)
        dd = lambda a_, b_: jnp.dot(a_, b_, preferred_element_type=F32)
        cl = dd(tri, hi) + dd(tri, md)
        c_mid = cl[mid:mid + 1, :]
        c_end = cl[L - 1:L, :]
        w_inv = jnp.exp(c_mid - cl)
        w_end = jnp.exp(c_end - cl)
        ch["e0"] = jnp.exp(c_mid)
        ch["wl_end"] = jnp.exp(c_end)
        ch["a_t"] = -kkn * jnp.exp(cl - lw_c - c_mid)
        ch["r_t"] = r * jnp.exp(cl - c_mid)
        b_v = kkn * a
        ch["b_t"] = b_v * w_inv
        ch["k_t"] = kp * w_inv
        ch["b_h"] = b_v * w_end
        ch["k_h"] = kp * w_end
        ch["v"] = v

    ri = lax.broadcasted_iota(jnp.int32, (L, L), 0)
    ci = lax.broadcasted_iota(jnp.int32, (L, L), 1)
    tri = jnp.where(ri >= ci, 1.0, 0.0).astype(BF16)
    crow = lax.broadcasted_iota(jnp.int32, (L, gw), 0)
    ccol = lax.broadcasted_iota(jnp.int32, (L, gw), 1) % L
    cmask_s = crow > ccol
    cmask_i = crow >= ccol
    eye_c = crow == ccol
    eye_f = jnp.where(eye_c, 1.0, 0.0)
    glane = lax.broadcasted_iota(jnp.int32, (1, gw), 1)
    head_masks = [(glane >= h * head) & (glane < (h + 1) * head) for h in range(HEAD_GROUP)]
    mid = L // 2 - 1
    n_dbl = int(math.log2(L))

    def stack(x):
        xb = x.astype(BF16)
        zero = jnp.zeros_like(xb)
        return jnp.concatenate([jnp.where(m, xb, zero) for m in head_masks], axis=0)

    def pair_products(ch):
        for gi in range(n_grp):
            ls = slice(gi * gw, (gi + 1) * gw)
            st = ch["grp"][gi]
            a_t, st["r_t"] = ch["a_t"][:, ls], ch["r_t"][:, ls]
            st["e0"], st["wl_end"] = ch["e0"][:, ls], ch["wl_end"][:, ls]
            st["sv"] = stack(ch["v"][:, ls])
            st["sa"] = stack(a_t)
            m4 = _dot_nt(jnp.concatenate([a_t, st["r_t"]], axis=0),
                         jnp.concatenate([stack(ch["b_t"][:, ls]), stack(ch["k_t"][:, ls])], axis=0))
            st["c_ab"] = jnp.where(cmask_s, m4[0:L, 0:gw], 0.0)
            st["c_ak"] = jnp.where(cmask_s, m4[0:L, gw:2 * gw], 0.0)
            st["c_rb"] = jnp.where(cmask_i, m4[L:2 * L, 0:gw], 0.0)
            st["c_rk"] = jnp.where(cmask_i, m4[L:2 * L, gw:2 * gw], 0.0)
            st["ct"] = _dot_nt(eye_f, jnp.concatenate([stack(ch["b_h"][:, ls]), stack(ch["k_h"][:, ls])],
                                                      axis=0))

    def square_and_v(ch):
        for st in ch["grp"]:
            st["s_pow"] = _dot(st["c_ab"], stack(st["c_ab"]))
            st["t_c"] = jnp.where(eye_c, 1.0, st["c_ab"])
            xv = _dot(jnp.concatenate([st["c_ak"], st["c_rk"], st["ct"][:, gw:2 * gw]], axis=0), st["sv"])
            st["v2"] = xv[0:L]
            st["rk_v"] = xv[L:2 * L]
            st["kh_v"] = xv[2 * L:3 * L]

    def doubling(it):
        def run(ch):
            for st in ch["grp"]:
                bd = stack(st["s_pow"])
                if it < n_dbl - 1:
                    prod = _dot(jnp.concatenate([st["t_c"], st["s_pow"]], axis=0), bd)
                    st["t_c"] = st["t_c"] + prod[0:L]
                    st["s_pow"] = prod[L:2 * L]
                else:
                    st["t_c"] = st["t_c"] + _dot(st["t_c"], bd)
        return run

    def solve(ch):
        for st in ch["grp"]:
            x2 = _dot(st["t_c"], jnp.concatenate([st["sa"], stack(st["v2"])], axis=1))
            st["s_x2"] = jnp.concatenate([stack(x2[:, 0:gw]), stack(x2[:, gw:2 * gw])], axis=1)

    def transfer(ch):
        for st in ch["grp"]:
            wp = _dot(jnp.concatenate([st["c_rb"], st["ct"][:, 0:gw]], axis=0), st["s_x2"])
            w1, pq = wp[0:L], wp[L:2 * L]
            rhat = (st["r_t"] + w1[:, 0:gw]) * st["e0"]
            p_c = jnp.where(eye_c, st["wl_end"], 0.0) + pq[:, 0:gw] * st["e0"]
            st["rp"] = jnp.concatenate([rhat, p_c], axis=0)
            st["y0"] = w1[:, gw:2 * gw] + st["rk_v"]
            st["q_c"] = pq[:, gw:2 * gw] + st["kh_v"]

    def state_pass(ch):
        ys = []
        for gi, st in enumerate(ch["grp"]):
            yh = _dot(st["rp"], stack(h_s[gi]))
            ys.append(yh[0:L] + st["y0"])
            h_s[gi] = yh[L:2 * L] + st["q_c"]
        ch["y"] = jnp.concatenate(ys, axis=1)

    def finish(ch):
        y = ch["y"]
        mean = _head_sum(y, head) * (1.0 / head)
        dlt = y - mean
        var = _head_sum(dlt * dlt, head) * (1.0 / head)
        yn = dlt * lax.rsqrt(var + RW_GN_EPS) * gn_g + gn_b
        yg_ref[ch["rows"], :] = ((yn + ch["bonus"]) * ch["g"]).astype(yg_ref.dtype)

    stages = [prepare, pair_products, square_and_v] + [doubling(it) for it in range(1, n_dbl)]
    stages += [solve, transfer, state_pass, finish]
    chunks = [dict(c=c, rows=slice(c * L, (c + 1) * L), grp=[dict() for _ in range(n_grp)])
              for c in range(n_chunk)]
    for step in range(n_chunk + len(stages) - 1):
        for ch in chunks:
            if 0 <= step - ch["c"] < len(stages):
                stages[step - ch["c"]](ch)

    zbuf[0:SUBLANES, :] = z_ref[tm - SUBLANES:tm, :]


def _rwkv_call(z, vfirst, mu, wa2, g2, v2, vecs, tm, head):
    bsz, s, zw = z.shape
    rw = g2.shape[1]
    n_tiles = s // tm
    has_vres = vfirst is not None
    c2 = lambda b, i: (0, 0)
    tile = lambda w: pl.BlockSpec((None, tm, w), lambda b, i: (b, i, 0))
    in_specs = [tile(zw)]
    args = [z]
    if has_vres:
        in_specs.append(tile(rw))
        args.append(vfirst)
    in_specs += [pl.BlockSpec(mu.shape, c2), pl.BlockSpec(wa2.shape, c2), pl.BlockSpec(g2.shape, c2)]
    args += [mu, wa2, g2]
    if has_vres:
        in_specs.append(pl.BlockSpec(v2.shape, c2))
        args.append(v2)
    in_specs.append(pl.BlockSpec(vecs.shape, c2))
    args.append(vecs)
    out_specs = [tile(rw)]
    out_shape = [jax.ShapeDtypeStruct((bsz, s, rw), BF16)]
    if not has_vres:
        out_specs.append(tile(rw))
        out_shape.append(jax.ShapeDtypeStruct((bsz, s, rw), F32))
    gw = HEAD_GROUP * head
    scratch = [pltpu.VMEM((CHUNK + SUBLANES, zw), F32),
               pltpu.VMEM((rw // gw, CHUNK, gw), F32)]
    kern = functools.partial(_rwkv_kernel, tm=tm, rw=rw, head=head, has_vres=has_vres)
    res = pl.pallas_call(
        kern,
        grid=(bsz, n_tiles),
        in_specs=in_specs,
        out_specs=out_specs,
        out_shape=out_shape,
        scratch_shapes=scratch,
        compiler_params=pltpu.CompilerParams(
            dimension_semantics=("parallel", "arbitrary"), vmem_limit_bytes=VMEM_LIMIT_BYTES),
        name="rwkv_vres" if has_vres else "rwkv",
    )(*args)
    return (res[0], vfirst) if has_vres else (res[0], res[1])


def _back_kernel(yg_ref, g0_ref, m12_ref, x_ref, ada_ref, wo_ref, wout_ref, lnm_ref, w1_ref, w2_ref, lnf_ref,
                 o_ref, *, alpha, n_split):
    y_rw = jnp.dot(yg_ref[...], wo_ref[...], preferred_element_type=F32)
    merged = _sigmoid(g0_ref[...].astype(F32)) * y_rw + m12_ref[...]
    o = _dot(merged, wout_ref[...])
    x = _layernorm(alpha * x_ref[...] + ada_ref[2:3, :] * o, lnm_ref[0:1, :], lnm_ref[1:2, :], LN_EPS)

    sh = ada_ref[3:4, :]
    sc = ada_ref[4:5, :]
    gt = ada_ref[5:6, :]
    h = (x * (1.0 + sc) + sh).astype(BF16)
    dff = w1_ref.shape[1]
    step = dff // n_split
    acc = None
    for j in range(n_split):
        t = jnp.dot(h, w1_ref[:, j * step:(j + 1) * step], preferred_element_type=F32)
        t = jnp.square(jnp.maximum(t, 0.0)).astype(BF16)
        part = jnp.dot(t, w2_ref[j * step:(j + 1) * step, :], preferred_element_type=F32)
        acc = part if acc is None else acc + part
    o_ref[...] = _layernorm(alpha * x + gt * acc, lnf_ref[0:1, :], lnf_ref[1:2, :], LN_EPS)


def _back_call(yg, g0, m12, x, ada, wo, wout, lnm, w1, w2, lnf, tm, alpha):
    bsz, s, d = x.shape
    dff = w1.shape[1]
    c2 = lambda b, i: (0, 0)
    tile = lambda w: pl.BlockSpec((None, tm, w), lambda b, i: (b, i, 0))
    resident = lambda a: pl.BlockSpec(a.shape, c2, pipeline_mode=pl.Buffered(1))
    kern = functools.partial(_back_kernel, alpha=alpha, n_split=dff // d)
    return pl.pallas_call(
        kern,
        grid=(bsz, s // tm),
        in_specs=[
            tile(yg.shape[-1]), tile(d), tile(d), tile(d),
            pl.BlockSpec((None,) + ada.shape[1:], lambda b, i: (b, 0, 0)),
            resident(wo), resident(wout), pl.BlockSpec(lnm.shape, c2),
            resident(w1), resident(w2), pl.BlockSpec(lnf.shape, c2),
        ],
        out_specs=tile(d),
        out_shape=jax.ShapeDtypeStruct((bsz, s, d), F32),
        compiler_params=pltpu.CompilerParams(
            dimension_semantics=("parallel", "arbitrary"), vmem_limit_bytes=VMEM_LIMIT_BYTES),
        name="back",
    )(yg, g0, m12, x, ada, wo, wout, lnm, w1, w2, lnf)


def _pick_tile(s, pref):
    tm = min(pref, s)
    assert s % tm == 0 and tm % CHUNK == 0, (s, tm)
    return tm


def kernel(x, c, ada_w, ada_b, w_in, w_in_vres, shift_mu, shift_mu_vres, rw_w0, rw_w2, rw_a0, rw_a2, rw_g2, rw_v0, rw_v2, rw_kk, rw_ka, rw_rk, rw_gn_g, rw_gn_b, rw_wo, cv_w, cv_b, cv_ln_g, cv_ln_b, cv_wo, pl_w, pl_scale, pl_wo, w_out, ln_m_g, ln_m_b, mlp_w1, mlp_w2, ln_f_g, ln_f_b):
    bsz, s, d = x.shape
    depth = ada_w.shape[0]
    heads, head = rw_rk.shape[1], rw_rk.shape[2]
    rw = heads * head
    lw_w, la_w, lg_w = rw_w2.shape[1], rw_a2.shape[1], rw_g2.shape[1]
    lv_w = rw_v2.shape[1]
    cv = cv_w.shape[2]
    pw = pl_scale.shape[1]
    assert lw_w + la_w == LANES and lg_w == LANES and lv_w <= LANES and LANES % head == 0
    assert rw % (HEAD_GROUP * head) == 0 and head == CHUNK
    alpha = float((2 * depth) ** 0.25)
    rw_shift = 3 * rw + lw_w + la_w + lg_w
    cv_off, pl_off = rw_shift, rw_shift + 2 * cv
    gt_off = pl_off + pw

    tm_in = _pick_tile(s, 512)
    tm_rw = _pick_tile(s, 1024)
    tm_ff = _pick_tile(s, 512)

    ada = _ada_call(c, ada_w, ada_b)

    v_first = None
    for l in range(depth):
        has_vres = l > 0
        w_l = w_in[l]
        wz = w_l[:, :rw_shift]
        mu = shift_mu[l]
        if has_vres:
            pad = LANES - lv_w
            wz = jnp.concatenate([wz, w_in_vres[l - 1], jnp.zeros((d, pad), F32)], axis=1)
            mu = jnp.concatenate([mu, shift_mu_vres[l - 1], jnp.zeros((pad,), F32)])
        wu = w_l[:, cv_off:pl_off].reshape(d, 2, cv // LANES, LANES).transpose(0, 2, 1, 3).reshape(d, 2 * cv)
        wz, wu, wp, wg = (t.astype(BF16) for t in (wz, wu, w_l[:, pl_off:gt_off], w_l[:, gt_off:]))
        z, g, m12 = _front_call(x, ada[l], wz, wu, wp, wg, cv_w[l], cv_b[l], cv_ln_g[l], cv_ln_b[l],
                                cv_wo[l].astype(BF16), pl_w[l].astype(BF16), pl_scale[l], pl_wo[l].astype(BF16),
                                tm_in)

        wa2 = jnp.zeros((LANES, 2 * rw), F32)
        wa2 = wa2.at[:lw_w, :rw].set(rw_w2[l]).at[lw_w:, rw:].set(rw_a2[l]).astype(BF16)
        v0 = rw_v0[l - 1] if has_vres else jnp.zeros((rw,), F32)
        vecs = jnp.stack([rw_w0[l], rw_a0[l], rw_kk[l], rw_ka[l], rw_rk[l].reshape(rw),
                          rw_gn_g[l], rw_gn_b[l], v0])
        v2 = None
        if has_vres:
            v2 = jnp.zeros((LANES, rw), F32).at[:lv_w].set(rw_v2[l - 1]).astype(BF16)
        yg, v_first = _rwkv_call(z, v_first, mu.reshape(1, -1), wa2, rw_g2[l].astype(BF16), v2, vecs, tm_rw, head)

        lnm = jnp.stack([ln_m_g[l], ln_m_b[l]])
        lnf = jnp.stack([ln_f_g[l], ln_f_b[l]])
        x = _back_call(yg, g, m12, x, ada[l], rw_wo[l].astype(BF16), w_out[l].astype(BF16), lnm,
                       mlp_w1[l].astype(BF16), mlp_w2[l].astype(BF16), lnf, tm_ff, alpha)
    return x
```

```python
import functools
import math

import jax
import jax.numpy as jnp
from jax import lax
from jax.experimental import pallas as pl
from jax.experimental.pallas import tpu as pltpu

F32 = jnp.float32
BF16 = jnp.bfloat16

LN_EPS = 1e-5
RW_GN_EPS = 64e-5
POOL_WINDOWS = (2, 4, 8, 16)
KK_NORM_FLOOR = 1e-12
DECAY_SCALE = math.exp(-0.5)

LANES = 128
SUBLANES = 8
VMEM_LIMIT_BYTES = 56 * 1024 * 1024

CHUNK = 64
HEAD_GROUP = 4


def _dot(a, b):
    return jnp.dot(a.astype(BF16), b.astype(BF16), preferred_element_type=F32)


def _dot_nt(a, b):
    return lax.dot_general(a.astype(BF16), b.astype(BF16), (((1,), (1,)), ((), ())),
                           preferred_element_type=F32)


def _split3(x):
    hi = x.astype(BF16)
    r1 = x - hi.astype(F32)
    mid = r1.astype(BF16)
    lo = (r1 - mid.astype(F32)).astype(BF16)
    return hi, mid, lo


def _layernorm(x, g, b, eps):
    mu = jnp.mean(x, axis=-1, keepdims=True)
    d = x - mu
    var = jnp.mean(d * d, axis=-1, keepdims=True)
    return d * lax.rsqrt(var + eps) * g + b


def _sigmoid(x):
    return 0.5 * jnp.tanh(0.5 * x) + 0.5


def _ada_kernel(c_ref, w_ref, b_ref, o_ref):
    c = c_ref[...]
    cond = c * _sigmoid(c)
    c_hi, c_mid, c_lo = _split3(cond)
    w_hi, w_mid, w_lo = _split3(w_ref[...])
    dd = lambda a, b: jnp.dot(a, b, preferred_element_type=F32)
    acc = dd(c_hi, w_hi) + (dd(c_hi, w_mid) + dd(c_mid, w_hi)) + (dd(c_hi, w_lo) + dd(c_mid, w_mid) + dd(c_lo, w_hi))
    o_ref[...] = acc + b_ref[...]


def _ada_call(c, ada_w, ada_b):
    depth, d, nd = ada_w.shape
    n_ada = nd // d
    bsz = c.shape[0]
    out = pl.pallas_call(
        _ada_kernel,
        grid=(depth, n_ada),
        in_specs=[
            pl.BlockSpec((bsz, d), lambda l, j: (0, 0)),
            pl.BlockSpec((None, d, d), lambda l, j: (l, 0, j)),
            pl.BlockSpec((None, None, 1, d), lambda l, j: (l, j, 0, 0)),
        ],
        out_specs=pl.BlockSpec((None, None, bsz, d), lambda l, j: (l, j, 0, 0)),
        out_shape=jax.ShapeDtypeStruct((depth, n_ada, bsz, d), F32),
        name="ada",
    )(c, ada_w, ada_b.reshape(depth, n_ada, 1, d))
    return jnp.transpose(out, (0, 2, 1, 3))


CONV_ROW_BLOCK = 32
PROJ_COL_CHUNK = 512


def _front_kernel(x_ref, ada_ref, wz_ref, wu_ref, wp_ref, wg_ref, cvw_ref, cvb_ref, lng_ref, lnb_ref, cvwo_ref,
                  plw_ref, pls_ref, plwo_ref, z_ref, g0_ref, m12_ref, hbuf, shbuf, pbuf, acc_s, gate_s,
                  *, tm, cv, conv_width, pg):
    i = pl.program_id(1)
    d = g0_ref.shape[1]
    halo_c = hbuf.shape[0] - tm
    halo_p = pbuf.shape[0] - tm

    @pl.when(i == 0)
    def _():
        hbuf[0:halo_c, :] = jnp.zeros((halo_c, cv), F32)
        pbuf[0:halo_p, :] = jnp.zeros((halo_p, pbuf.shape[1]), F32)

    sh = ada_ref[0:1, :]
    sc = ada_ref[1:2, :]
    h = (x_ref[...] * (1.0 + sc) + sh).astype(BF16)

    def u_job(sl):
        def run():
            u = jnp.dot(h, wu_ref[:, 2 * sl * LANES:2 * (sl + 1) * LANES], preferred_element_type=F32)
            hbuf[halo_c:halo_c + tm, sl * LANES:(sl + 1) * LANES] = u[:, :LANES] * _sigmoid(u[:, LANES:])
        return run

    def p_job():
        pbuf[halo_p:halo_p + tm, :] = jnp.dot(h, wp_ref[...], preferred_element_type=F32)

    def z_job(c0, c1):
        def run():
            z_ref[:, c0:c1] = jnp.dot(h, wz_ref[:, c0:c1], preferred_element_type=F32).astype(z_ref.dtype)
        return run

    def g_job(c0, c1):
        def run():
            gv = jnp.dot(h, wg_ref[:, c0:c1], preferred_element_type=F32)
            if c0 < d:
                g0_ref[:, c0:c1] = gv.astype(g0_ref.dtype)
            else:
                gate_s[:, c0 - d:c1 - d] = _sigmoid(gv)
        return run

    zw = z_ref.shape[1]
    jobs = [p_job]
    jobs += [z_job(c0, min(c0 + PROJ_COL_CHUNK, zw)) for c0 in range(0, zw, PROJ_COL_CHUNK)]
    jobs += [g_job(c0, c0 + PROJ_COL_CHUNK) for c0 in range(0, wg_ref.shape[1], PROJ_COL_CHUNK)]

    base = halo_c - (conv_width - 1)
    span = shbuf.shape[2]
    n_slab = cv // LANES
    n_jobs = len(jobs)
    for sl in range(n_slab):
        u_job(sl)()
        cs = slice(sl * LANES, (sl + 1) * LANES)
        sb = shbuf.at[sl % 2]
        for r in range(1, SUBLANES):
            sb[r - 1] = hbuf[r:r + span, cs]
        for rb in range(tm // CONV_ROW_BLOCK):
            r0 = rb * CONV_ROW_BLOCK
            acc = None
            for j in range(conv_width):
                q, r = divmod(base + j, SUBLANES)
                lo = q * SUBLANES + r0
                src = hbuf[lo:lo + CONV_ROW_BLOCK, cs] if r == 0 else sb[r - 1, lo:lo + CONV_ROW_BLOCK, :]
                term = src * cvw_ref[j:j + 1, cs]
                acc = term if acc is None else acc + term
            acc_s[r0:r0 + CONV_ROW_BLOCK, cs] = acc + cvb_ref[:, cs]
        while jobs and (n_jobs - len(jobs)) * n_slab < (sl + 1) * n_jobs:
            jobs.pop(0)()
    assert not jobs
    hbuf[0:halo_c, :] = hbuf[tm:tm + halo_c, :]

    hn = _layernorm(acc_s[...], lng_ref[...], lnb_ref[...], LN_EPS)
    hn = hn * _sigmoid(hn)
    y_cv = _dot(hn, cvwo_ref[...])

    t1 = (lax.broadcasted_iota(jnp.int32, (tm, 1), 0) + (i * tm + 1)).astype(F32)
    mixed = []
    for gi, win in enumerate(POOL_WINDOWS):
        ps = slice(gi * pg, (gi + 1) * pg)
        ssum = pbuf[halo_p:halo_p + tm, ps]
        for dlt in range(1, win):
            ssum = ssum + pbuf[halo_p - dlt:halo_p - dlt + tm, ps]
        pooled = ssum / jnp.minimum(t1, float(win)) - pbuf[halo_p:halo_p + tm, ps]
        mixed.append(_dot(pooled, plw_ref[gi]))
    mixed = jnp.concatenate(mixed, axis=1) * pls_ref[...]
    y_pl = _dot(mixed, plwo_ref[...])
    pbuf[0:halo_p, :] = pbuf[tm:tm + halo_p, :]

    m12_ref[...] = gate_s[:, 0:d] * y_cv + gate_s[:, d:2 * d] * y_pl


def _front_call(x, ada, wz, wu, wp, wg, cvw, cvb, lng, lnb, cvwo, plw, pls, plwo, tm):
    bsz, s, d = x.shape
    conv_width, cv = cvw.shape
    pw = wp.shape[1]
    pg = plw.shape[1]
    zw = wz.shape[1]
    halo_c = -(-(conv_width - 1) // SUBLANES) * SUBLANES
    halo_p = -(-(max(POOL_WINDOWS) - 1) // SUBLANES) * SUBLANES
    c2 = lambda b, i: (0, 0)
    tile = lambda w: pl.BlockSpec((None, tm, w), lambda b, i: (b, i, 0))
    resident = lambda a: pl.BlockSpec(a.shape, c2, pipeline_mode=pl.Buffered(1))
    kern = functools.partial(_front_kernel, tm=tm, cv=cv, conv_width=conv_width, pg=pg)
    return pl.pallas_call(
        kern,
        grid=(bsz, s // tm),
        in_specs=[
            tile(d),
            pl.BlockSpec((None,) + ada.shape[1:], lambda b, i: (b, 0, 0)),
            resident(wz), resident(wu), resident(wp), resident(wg),
            pl.BlockSpec(cvw.shape, c2),
            pl.BlockSpec((1, cv), c2),
            pl.BlockSpec((1, cv), c2),
            pl.BlockSpec((1, cv), c2),
            resident(cvwo),
            pl.BlockSpec(plw.shape, lambda b, i: (0, 0, 0)),
            pl.BlockSpec((1, pw), c2),
            resident(plwo),
        ],
        out_specs=[tile(zw), tile(d), tile(d)],
        out_shape=[jax.ShapeDtypeStruct((bsz, s, zw), F32), jax.ShapeDtypeStruct((bsz, s, d), BF16),
                   jax.ShapeDtypeStruct((bsz, s, d), F32)],
        scratch_shapes=[pltpu.VMEM((tm + halo_c, cv), F32),
                        pltpu.VMEM((2, SUBLANES - 1, tm + halo_c - SUBLANES, LANES), F32),
                        pltpu.VMEM((tm + halo_p, pw), F32),
                        pltpu.VMEM((tm, cv), F32),
                        pltpu.VMEM((tm, 2 * d), F32)],
        compiler_params=pltpu.CompilerParams(
            dimension_semantics=("parallel", "arbitrary"), vmem_limit_bytes=VMEM_LIMIT_BYTES),
        name="front",
    )(x, ada, wz, wu, wp, wg, cvw, cvb.reshape(1, cv), lng.reshape(1, cv), lnb.reshape(1, cv), cvwo,
      plw, pls.reshape(1, pw), plwo)


def _head_sum(x, head):
    assert LANES == 2 * head
    low = lax.broadcasted_iota(jnp.int32, (1, LANES), 1) < head
    out = []
    for s0 in range(0, x.shape[1], LANES):
        slab = x[:, s0:s0 + LANES]
        tot_low = jnp.sum(jnp.where(low, slab, 0.0), axis=-1, keepdims=True)
        tot_high = jnp.sum(slab, axis=-1, keepdims=True) - tot_low
        out.append(jnp.where(low, tot_low, tot_high))
    return jnp.concatenate(out, axis=1)


def _rwkv_kernel(*refs, tm, rw, head, has_vres):
    if has_vres:
        (z_ref, vf_ref, mu_ref, wa2_ref, g2_ref, v2_ref, vec_ref, yg_ref, zbuf, h_s) = refs
        vfo_ref = None
    else:
        (z_ref, mu_ref, wa2_ref, g2_ref, vec_ref, yg_ref, vfo_ref, zbuf, h_s) = refs
        vf_ref = v2_ref = None
    i = pl.program_id(1)
    gw = HEAD_GROUP * head
    n_grp = rw // gw
    L = CHUNK
    n_chunk = tm // L
    w0, a0, k_k, k_a, r_k, gn_g, gn_b, v0 = (vec_ref[j:j + 1, :] for j in range(8))
    lane = lax.broadcasted_iota(jnp.int32, (1, LANES), 1)

    @pl.when(i == 0)
    def _():
        h_s[...] = jnp.zeros(h_s.shape, F32)
        zbuf[0:SUBLANES, :] = jnp.zeros((SUBLANES, zbuf.shape[1]), F32)

    def prepare(ch):
        rows = ch["rows"]
        p = ch["c"]

        def shifted(c0, c1):
            zr = z_ref[rows, c0:c1]
            if p == 0:
                zbuf[SUBLANES:SUBLANES + L, c0:c1] = zr
                prev = zbuf[SUBLANES - 1:SUBLANES - 1 + L, c0:c1]
            else:
                prev = z_ref[p * L - 1:(p + 1) * L - 1, c0:c1]
            return zr + (prev - zr) * mu_ref[:, c0:c1]

        lwa = shifted(3 * rw, 3 * rw + LANES)
        lora_in = jnp.where(lane < LANES // 2, jnp.tanh(lwa), lwa)
        wl = _dot(lora_in, wa2_ref[...])
        lw_c = -DECAY_SCALE * _sigmoid(w0 + wl[:, 0:rw])
        a = _sigmoid(a0 + wl[:, rw:2 * rw])
        lg = shifted(3 * rw + LANES, 3 * rw + 2 * LANES)
        ch["g"] = _dot(_sigmoid(lg), g2_ref[...])
        v = shifted(2 * rw, 3 * rw)
        if has_vres:
            lv = shifted(3 * rw + 2 * LANES, 3 * rw + 3 * LANES)
            v = v + (vf_ref[rows, :] - v) * _sigmoid(v0 + _dot(lv, v2_ref[...]))
        else:
            vfo_ref[rows, :] = v
        k = shifted(rw, 2 * rw)
        kk = k * k_k
        kkn = kk * jnp.minimum(lax.rsqrt(_head_sum(kk * kk, head)), 1.0 / KK_NORM_FLOOR)
        kp = k * (1.0 + (a - 1.0) * k_a)
        r = shifted(0, rw)
        ch["bonus"] = _head_sum(r * kp * r_k, head) * v
        hi, md, _ = _split3(lw_c)
        dd = lambda a_, b_: jnp.dot(a_, b_, preferred_element_type=F32)
        cl = dd(tri, hi) + dd(tri, md)
        c_mid = cl[mid:mid + 1, :]
        c_end = cl[L - 1:L, :]
        w_inv = jnp.exp(c_mid - cl)
        w_end = jnp.exp(c_end - cl)
        ch["e0"] = jnp.exp(c_mid)
        ch["wl_end"] = jnp.exp(c_end)
        ch["a_t"] = -kkn * jnp.exp(cl - lw_c - c_mid)
        ch["r_t"] = r * jnp.exp(cl - c_mid)
        b_v = kkn * a
        ch["b_t"] = b_v * w_inv
        ch["k_t"] = kp * w_inv
        ch["b_h"] = b_v * w_end
        ch["k_h"] = kp * w_end
        ch["v"] = v

    ri = lax.broadcasted_iota(jnp.int32, (L, L), 0)
    ci = lax.broadcasted_iota(jnp.int32, (L, L), 1)
    tri = jnp.where(ri >= ci, 1.0, 0.0).astype(BF16)
    crow = lax.broadcasted_iota(jnp.int32, (L, gw), 0)
    ccol = lax.broadcasted_iota(jnp.int32, (L, gw), 1) % L
    cmask_s = crow > ccol
    cmask_i = crow >= ccol
    eye_c = crow == ccol
    eye_f = jnp.where(eye_c, 1.0, 0.0)
    glane = lax.broadcasted_iota(jnp.int32, (1, gw), 1)
    head_masks = [(glane >= h * head) & (glane < (h + 1) * head) for h in range(HEAD_GROUP)]
    mid = L // 2 - 1
    n_dbl = int(math.log2(L))

    def stack(x):
        xb = x.astype(BF16)
        zero = jnp.zeros_like(xb)
        return jnp.concatenate([jnp.where(m, xb, zero) for m in head_masks], axis=0)

    def pair_products(ch):
        for gi in range(n_grp):
            ls = slice(gi * gw, (gi + 1) * gw)
            st = ch["grp"][gi]
            a_t, st["r_t"] = ch["a_t"][:, ls], ch["r_t"][:, ls]
            st["e0"], st["wl_end"] = ch["e0"][:, ls], ch["wl_end"][:, ls]
            st["sv"] = stack(ch["v"][:, ls])
            st["sa"] = stack(a_t)
            m4 = _dot_nt(jnp.concatenate([a_t, st["r_t"]], axis=0),
                         jnp.concatenate([stack(ch["b_t"][:, ls]), stack(ch["k_t"][:, ls])], axis=0))
            st["c_ab"] = jnp.where(cmask_s, m4[0:L, 0:gw], 0.0)
            st["c_ak"] = jnp.where(cmask_s, m4[0:L, gw:2 * gw], 0.0)
            st["c_rb"] = jnp.where(cmask_i, m4[L:2 * L, 0:gw], 0.0)
            st["c_rk"] = jnp.where(cmask_i, m4[L:2 * L, gw:2 * gw], 0.0)
            st["ct"] = _dot_nt(eye_f, jnp.concatenate([stack(ch["b_h"][:, ls]), stack(ch["k_h"][:, ls])],
                                                      axis=0))

    def square_and_v(ch):
        for st in ch["grp"]:
            st["s_pow"] = _dot(st["c_ab"], stack(st["c_ab"]))
            st["t_c"] = jnp.where(eye_c, 1.0, st["c_ab"])
            xv = _dot(jnp.concatenate([st["c_ak"], st["c_rk"], st["ct"][:, gw:2 * gw]], axis=0), st["sv"])
            st["v2"] = xv[0:L]
            st["rk_v"] = xv[L:2 * L]
            st["kh_v"] = xv[2 * L:3 * L]

    def doubling(it):
        def run(ch):
            for st in ch["grp"]:
                bd = stack(st["s_pow"])
                if it < n_dbl - 1:
                    prod = _dot(jnp.concatenate([st["t_c"], st["s_pow"]], axis=0), bd)
                    st["t_c"] = st["t_c"] + prod[0:L]
                    st["s_pow"] = prod[L:2 * L]
                else:
                    st["t_c"] = st["t_c"] + _dot(st["t_c"], bd)
        return run

    def solve(ch):
        for st in ch["grp"]:
            x2 = _dot(st["t_c"], jnp.concatenate([st["sa"], stack(st["v2"])], axis=1))
            st["s_x2"] = jnp.concatenate([stack(x2[:, 0:gw]), stack(x2[:, gw:2 * gw])], axis=1)

    def transfer(ch):
        for st in ch["grp"]:
            wp = _dot(jnp.concatenate([st["c_rb"], st["ct"][:, 0:gw]], axis=0), st["s_x2"])
            w1, pq = wp[0:L], wp[L:2 * L]
            rhat = (st["r_t"] + w1[:, 0:gw]) * st["e0"]
            p_c = jnp.where(eye_c, st["wl_end"], 0.0) + pq[:, 0:gw] * st["e0"]
            st["rp"] = jnp.concatenate([rhat, p_c], axis=0)
            st["y0"] = w1[:, gw:2 * gw] + st["rk_v"]
            st["q_c"] = pq[:, gw:2 * gw] + st["kh_v"]

    def state_pass(ch):
        ys = []
        for gi, st in enumerate(ch["grp"]):
            yh = _dot(st["rp"], stack(h_s[gi]))
            ys.append(yh[0:L] + st["y0"])
            h_s[gi] = yh[L:2 * L] + st["q_c"]
        ch["y"] = jnp.concatenate(ys, axis=1)

    def finish(ch):
        y = ch["y"]
        mean = _head_sum(y, head) * (1.0 / head)
        dlt = y - mean
        var = _head_sum(dlt * dlt, head) * (1.0 / head)
        yn = dlt * lax.rsqrt(var + RW_GN_EPS) * gn_g + gn_b
        yg_ref[ch["rows"], :] = ((yn + ch["bonus"]) * ch["g"]).astype(yg_ref.dtype)

    stages = [prepare, pair_products, square_and_v] + [doubling(it) for it in range(1, n_dbl)]
    stages += [solve, transfer, state_pass, finish]
    chunks = [dict(c=c, rows=slice(c * L, (c + 1) * L), grp=[dict() for _ in range(n_grp)])
              for c in range(n_chunk)]
    for step in range(n_chunk + len(stages) - 1):
        for ch in chunks:
            if 0 <= step - ch["c"] < len(stages):
                stages[step - ch["c"]](ch)

    zbuf[0:SUBLANES, :] = z_ref[tm - SUBLANES:tm, :]


def _rwkv_call(z, vfirst, mu, wa2, g2, v2, vecs, tm, head):
    bsz, s, zw = z.shape
    rw = g2.shape[1]
    n_tiles = s // tm
    has_vres = vfirst is not None
    c2 = lambda b, i: (0, 0)
    tile = lambda w: pl.BlockSpec((None, tm, w), lambda b, i: (b, i, 0))
    in_specs = [tile(zw)]
    args = [z]
    if has_vres:
        in_specs.append(tile(rw))
        args.append(vfirst)
    in_specs += [pl.BlockSpec(mu.shape, c2), pl.BlockSpec(wa2.shape, c2), pl.BlockSpec(g2.shape, c2)]
    args += [mu, wa2, g2]
    if has_vres:
        in_specs.append(pl.BlockSpec(v2.shape, c2))
        args.append(v2)
    in_specs.append(pl.BlockSpec(vecs.shape, c2))
    args.append(vecs)
    out_specs = [tile(rw)]
    out_shape = [jax.ShapeDtypeStruct((bsz, s, rw), BF16)]
    if not has_vres:
        out_specs.append(tile(rw))
        out_shape.append(jax.ShapeDtypeStruct((bsz, s, rw), F32))
    gw = HEAD_GROUP * head
    scratch = [pltpu.VMEM((CHUNK + SUBLANES, zw), F32),
               pltpu.VMEM((rw // gw, CHUNK, gw), F32)]
    kern = functools.partial(_rwkv_kernel, tm=tm, rw=rw, head=head, has_vres=has_vres)
    res = pl.pallas_call(
        kern,
        grid=(bsz, n_tiles),
        in_specs=in_specs,
        out_specs=out_specs,
        out_shape=out_shape,
        scratch_shapes=scratch,
        compiler_params=pltpu.CompilerParams(
            dimension_semantics=("parallel", "arbitrary"), vmem_limit_bytes=VMEM_LIMIT_BYTES),
        name="rwkv_vres" if has_vres else "rwkv",
    )(*args)
    return (res[0], vfirst) if has_vres else (res[0], res[1])


def _back_kernel(yg_ref, g0_ref, m12_ref, x_ref, ada_ref, wo_ref, wout_ref, lnm_ref, w1_ref, w2_ref, lnf_ref,
                 o_ref, *, alpha, n_split):
    y_rw = jnp.dot(yg_ref[...], wo_ref[...], preferred_element_type=F32)
    merged = _sigmoid(g0_ref[...].astype(F32)) * y_rw + m12_ref[...]
    o = _dot(merged, wout_ref[...])
    x = _layernorm(alpha * x_ref[...] + ada_ref[2:3, :] * o, lnm_ref[0:1, :], lnm_ref[1:2, :], LN_EPS)

    sh = ada_ref[3:4, :]
    sc = ada_ref[4:5, :]
    gt = ada_ref[5:6, :]
    h = (x * (1.0 + sc) + sh).astype(BF16)
    dff = w1_ref.shape[1]
    step = dff // n_split
    acc = None
    for j in range(n_split):
        t = jnp.dot(h, w1_ref[:, j * step:(j + 1) * step], preferred_element_type=F32)
        t = jnp.square(jnp.maximum(t, 0.0)).astype(BF16)
        part = jnp.dot(t, w2_ref[j * step:(j + 1) * step, :], preferred_element_type=F32)
        acc = part if acc is None else acc + part
    o_ref[...] = _layernorm(alpha * x + gt * acc, lnf_ref[0:1, :], lnf_ref[1:2, :], LN_EPS)


def _back_call(yg, g0, m12, x, ada, wo, wout, lnm, w1, w2, lnf, tm, alpha):
    bsz, s, d = x.shape
    dff = w1.shape[1]
    c2 = lambda b, i: (0, 0)
    tile = lambda w: pl.BlockSpec((None, tm, w), lambda b, i: (b, i, 0))
    resident = lambda a: pl.BlockSpec(a.shape, c2, pipeline_mode=pl.Buffered(1))
    kern = functools.partial(_back_kernel, alpha=alpha, n_split=dff // d)
    return pl.pallas_call(
        kern,
        grid=(bsz, s // tm),
        in_specs=[
            tile(yg.shape[-1]), tile(d), tile(d), tile(d),
            pl.BlockSpec((None,) + ada.shape[1:], lambda b, i: (b, 0, 0)),
            resident(wo), resident(wout), pl.BlockSpec(lnm.shape, c2),
            resident(w1), resident(w2), pl.BlockSpec(lnf.shape, c2),
        ],
        out_specs=tile(d),
        out_shape=jax.ShapeDtypeStruct((bsz, s, d), F32),
        compiler_params=pltpu.CompilerParams(
            dimension_semantics=("parallel", "arbitrary"), vmem_limit_bytes=VMEM_LIMIT_BYTES),
        name="back",
    )(yg, g0, m12, x, ada, wo, wout, lnm, w1, w2, lnf)


def _pick_tile(s, pref):
    tm = min(pref, s)
    assert s % tm == 0 and tm % CHUNK == 0, (s, tm)
    return tm


def kernel(x, c, ada_w, ada_b, w_in, w_in_vres, shift_mu, shift_mu_vres, rw_w0, rw_w2, rw_a0, rw_a2, rw_g2, rw_v0, rw_v2, rw_kk, rw_ka, rw_rk, rw_gn_g, rw_gn_b, rw_wo, cv_w, cv_b, cv_ln_g, cv_ln_b, cv_wo, pl_w, pl_scale, pl_wo, w_out, ln_m_g, ln_m_b, mlp_w1, mlp_w2, ln_f_g, ln_f_b):
    bsz, s, d = x.shape
    depth = ada_w.shape[0]
    heads, head = rw_rk.shape[1], rw_rk.shape[2]
    rw = heads * head
    lw_w, la_w, lg_w = rw_w2.shape[1], rw_a2.shape[1], rw_g2.shape[1]
    lv_w = rw_v2.shape[1]
    cv = cv_w.shape[2]
    pw = pl_scale.shape[1]
    assert lw_w + la_w == LANES and lg_w == LANES and lv_w <= LANES and LANES % head == 0
    assert rw % (HEAD_GROUP * head) == 0 and head == CHUNK
    alpha = float((2 * depth) ** 0.25)
    rw_shift = 3 * rw + lw_w + la_w + lg_w
    cv_off, pl_off = rw_shift, rw_shift + 2 * cv
    gt_off = pl_off + pw

    tm_in = _pick_tile(s, 512)
    tm_rw = _pick_tile(s, 1024)
    tm_ff = _pick_tile(s, 512)

    ada = _ada_call(c, ada_w, ada_b)

    v_first = None
    for l in range(depth):
        has_vres = l > 0
        w_l = w_in[l]
        wz = w_l[:, :rw_shift]
        mu = shift_mu[l]
        if has_vres:
            pad = LANES - lv_w
            wz = jnp.concatenate([wz, w_in_vres[l - 1], jnp.zeros((d, pad), F32)], axis=1)
            mu = jnp.concatenate([mu, shift_mu_vres[l - 1], jnp.zeros((pad,), F32)])
        wu = w_l[:, cv_off:pl_off].reshape(d, 2, cv // LANES, LANES).transpose(0, 2, 1, 3).reshape(d, 2 * cv)
        wz, wu, wp, wg = (t.astype(BF16) for t in (wz, wu, w_l[:, pl_off:gt_off], w_l[:, gt_off:]))
        z, g, m12 = _front_call(x, ada[l], wz, wu, wp, wg, cv_w[l], cv_b[l], cv_ln_g[l], cv_ln_b[l],
                                cv_wo[l].astype(BF16), pl_w[l].astype(BF16), pl_scale[l], pl_wo[l].astype(BF16),
                                tm_in)

        wa2 = jnp.zeros((LANES, 2 * rw), F32)
        wa2 = wa2.at[:lw_w, :rw].set(rw_w2[l]).at[lw_w:, rw:].set(rw_a2[l]).astype(BF16)
        v0 = rw_v0[l - 1] if has_vres else jnp.zeros((rw,), F32)
        vecs = jnp.stack([rw_w0[l], rw_a0[l], rw_kk[l], rw_ka[l], rw_rk[l].reshape(rw),
                          rw_gn_g[l], rw_gn_b[l], v0])
        v2 = None
        if has_vres:
            v2 = jnp.zeros((LANES, rw), F32).at[:lv_w].set(rw_v2[l - 1]).astype(BF16)
        yg, v_first = _rwkv_call(z, v_first, mu.reshape(1, -1), wa2, rw_g2[l].astype(BF16), v2, vecs, tm_rw, head)

        lnm = jnp.stack([ln_m_g[l], ln_m_b[l]])
        lnf = jnp.stack([ln_f_g[l], ln_f_b[l]])
        x = _back_call(yg, g, m12, x, ada[l], rw_wo[l].astype(BF16), w_out[l].astype(BF16), lnm,
                       mlp_w1[l].astype(BF16), mlp_w2[l].astype(BF16), lnf, tm_ff, alpha)
    return x
```

```python
import functools
import math

import jax
import jax.numpy as jnp
from jax import lax
from jax.experimental import pallas as pl
from jax.experimental.pallas import tpu as pltpu

F32 = jnp.float32
BF16 = jnp.bfloat16

LN_EPS = 1e-5
RW_GN_EPS = 64e-5
POOL_WINDOWS = (2, 4, 8, 16)
KK_NORM_FLOOR = 1e-12
DECAY_SCALE = math.exp(-0.5)

LANES = 128
SUBLANES = 8
VMEM_LIMIT_BYTES = 56 * 1024 * 1024

CHUNK = 64
HEAD_GROUP = 4


def _dot(a, b):
    return jnp.dot(a.astype(BF16), b.astype(BF16), preferred_element_type=F32)


def _dot_nt(a, b):
    return lax.dot_general(a.astype(BF16), b.astype(BF16), (((1,), (1,)), ((), ())),
                           preferred_element_type=F32)


def _split3(x):
    hi = x.astype(BF16)
    r1 = x - hi.astype(F32)
    mid = r1.astype(BF16)
    lo = (r1 - mid.astype(F32)).astype(BF16)
    return hi, mid, lo


def _layernorm(x, g, b, eps):
    mu = jnp.mean(x, axis=-1, keepdims=True)
    d = x - mu
    var = jnp.mean(d * d, axis=-1, keepdims=True)
    return d * lax.rsqrt(var + eps) * g + b


def _sigmoid(x):
    return 0.5 * jnp.tanh(0.5 * x) + 0.5


def _ada_kernel(c_ref, w_ref, b_ref, o_ref):
    c = c_ref[...]
    cond = c * _sigmoid(c)
    c_hi, c_mid, c_lo = _split3(cond)
    w_hi, w_mid, w_lo = _split3(w_ref[...])
    dd = lambda a, b: jnp.dot(a, b, preferred_element_type=F32)
    acc = dd(c_hi, w_hi) + (dd(c_hi, w_mid) + dd(c_mid, w_hi)) + (dd(c_hi, w_lo) + dd(c_mid, w_mid) + dd(c_lo, w_hi))
    o_ref[...] = acc + b_ref[...]


def _ada_call(c, ada_w, ada_b):
    depth, d, nd = ada_w.shape
    n_ada = nd // d
    bsz = c.shape[0]
    out = pl.pallas_call(
        _ada_kernel,
        grid=(depth, n_ada),
        in_specs=[
            pl.BlockSpec((bsz, d), lambda l, j: (0, 0)),
            pl.BlockSpec((None, d, d), lambda l, j: (l, 0, j)),
            pl.BlockSpec((None, None, 1, d), lambda l, j: (l, j, 0, 0)),
        ],
        out_specs=pl.BlockSpec((None, None, bsz, d), lambda l, j: (l, j, 0, 0)),
        out_shape=jax.ShapeDtypeStruct((depth, n_ada, bsz, d), F32),
        name="ada",
    )(c, ada_w, ada_b.reshape(depth, n_ada, 1, d))
    return jnp.transpose(out, (0, 2, 1, 3))


CONV_ROW_BLOCK = 32
PROJ_COL_CHUNK = 512


def _front_kernel(x_ref, ada_ref, wz_ref, wu_ref, wp_ref, wg_ref, cvw_ref, cvb_ref, lng_ref, lnb_ref, cvwo_ref,
                  plw_ref, pls_ref, plwo_ref, z_ref, g0_ref, m12_ref, hbuf, shbuf, pbuf, acc_s, gate_s,
                  *, tm, cv, conv_width, pg):
    i = pl.program_id(1)
    d = g0_ref.shape[1]
    halo_c = hbuf.shape[0] - tm
    halo_p = pbuf.shape[0] - tm

    @pl.when(i == 0)
    def _():
        hbuf[0:halo_c, :] = jnp.zeros((halo_c, cv), F32)
        pbuf[0:halo_p, :] = jnp.zeros((halo_p, pbuf.shape[1]), F32)

    sh = ada_ref[0:1, :]
    sc = ada_ref[1:2, :]
    h = (x_ref[...] * (1.0 + sc) + sh).astype(BF16)

    def u_job(sl):
        def run():
            u = jnp.dot(h, wu_ref[:, 2 * sl * LANES:2 * (sl + 1) * LANES], preferred_element_type=F32)
            hbuf[halo_c:halo_c + tm, sl * LANES:(sl + 1) * LANES] = u[:, :LANES] * _sigmoid(u[:, LANES:])
        return run

    def p_job():
        pbuf[halo_p:halo_p + tm, :] = jnp.dot(h, wp_ref[...], preferred_element_type=F32)

    def z_job(c0, c1):
        def run():
            z_ref[:, c0:c1] = jnp.dot(h, wz_ref[:, c0:c1], preferred_element_type=F32).astype(z_ref.dtype)
        return run

    def g_job(c0, c1):
        def run():
            gv = jnp.dot(h, wg_ref[:, c0:c1], preferred_element_type=F32)
            if c0 < d:
                g0_ref[:, c0:c1] = gv.astype(g0_ref.dtype)
            else:
                gate_s[:, c0 - d:c1 - d] = _sigmoid(gv)
        return run

    zw = z_ref.shape[1]
    jobs = [p_job]
    jobs += [z_job(c0, min(c0 + PROJ_COL_CHUNK, zw)) for c0 in range(0, zw, PROJ_COL_CHUNK)]
    jobs += [g_job(c0, c0 + PROJ_COL_CHUNK) for c0 in range(0, wg_ref.shape[1], PROJ_COL_CHUNK)]

    base = halo_c - (conv_width - 1)
    span = shbuf.shape[2]
    n_slab = cv // LANES
    n_jobs = len(jobs)
    for sl in range(n_slab):
        u_job(sl)()
        cs = slice(sl * LANES, (sl + 1) * LANES)
        sb = shbuf.at[sl % 2]
        for r in range(1, SUBLANES):
            sb[r - 1] = hbuf[r:r + span, cs]
        for rb in range(tm // CONV_ROW_BLOCK):
            r0 = rb * CONV_ROW_BLOCK
            acc = None
            for j in range(conv_width):
                q, r = divmod(base + j, SUBLANES)
                lo = q * SUBLANES + r0
                src = hbuf[lo:lo + CONV_ROW_BLOCK, cs] if r == 0 else sb[r - 1, lo:lo + CONV_ROW_BLOCK, :]
                term = src * cvw_ref[j:j + 1, cs]
                acc = term if acc is None else acc + term
            acc_s[r0:r0 + CONV_ROW_BLOCK, cs] = acc + cvb_ref[:, cs]
        while jobs and (n_jobs - len(jobs)) * n_slab < (sl + 1) * n_jobs:
            jobs.pop(0)()
    assert not jobs
    hbuf[0:halo_c, :] = hbuf[tm:tm + halo_c, :]

    y_cv = []
    for rs in (slice(0, tm // 2), slice(tm // 2, tm)):
        hn = _layernorm(acc_s[rs, :], lng_ref[...], lnb_ref[...], LN_EPS)
        y_cv.append(_dot(hn * _sigmoid(hn), cvwo_ref[...]))
    y_cv = jnp.concatenate(y_cv, axis=0)

    t1 = (lax.broadcasted_iota(jnp.int32, (tm, 1), 0) + (i * tm + 1)).astype(F32)
    mixed = []
    for gi, win in enumerate(POOL_WINDOWS):
        ps = slice(gi * pg, (gi + 1) * pg)
        ssum = pbuf[halo_p:halo_p + tm, ps]
        for dlt in range(1, win):
            ssum = ssum + pbuf[halo_p - dlt:halo_p - dlt + tm, ps]
        pooled = ssum / jnp.minimum(t1, float(win)) - pbuf[halo_p:halo_p + tm, ps]
        mixed.append(_dot(pooled, plw_ref[gi]))
    mixed = jnp.concatenate(mixed, axis=1) * pls_ref[...]
    y_pl = _dot(mixed, plwo_ref[...])
    pbuf[0:halo_p, :] = pbuf[tm:tm + halo_p, :]

    m12_ref[...] = gate_s[:, 0:d] * y_cv + gate_s[:, d:2 * d] * y_pl


def _front_call(x, ada, wz, wu, wp, wg, cvw, cvb, lng, lnb, cvwo, plw, pls, plwo, tm):
    bsz, s, d = x.shape
    conv_width, cv = cvw.shape
    pw = wp.shape[1]
    pg = plw.shape[1]
    zw = wz.shape[1]
    halo_c = -(-(conv_width - 1) // SUBLANES) * SUBLANES
    halo_p = -(-(max(POOL_WINDOWS) - 1) // SUBLANES) * SUBLANES
    c2 = lambda b, i: (0, 0)
    tile = lambda w: pl.BlockSpec((None, tm, w), lambda b, i: (b, i, 0))
    resident = lambda a: pl.BlockSpec(a.shape, c2, pipeline_mode=pl.Buffered(1))
    kern = functools.partial(_front_kernel, tm=tm, cv=cv, conv_width=conv_width, pg=pg)
    return pl.pallas_call(
        kern,
        grid=(bsz, s // tm),
        in_specs=[
            tile(d),
            pl.BlockSpec((None,) + ada.shape[1:], lambda b, i: (b, 0, 0)),
            resident(wz), resident(wu), resident(wp), resident(wg),
            pl.BlockSpec(cvw.shape, c2),
            pl.BlockSpec((1, cv), c2),
            pl.BlockSpec((1, cv), c2),
            pl.BlockSpec((1, cv), c2),
            resident(cvwo),
            pl.BlockSpec(plw.shape, lambda b, i: (0, 0, 0)),
            pl.BlockSpec((1, pw), c2),
            resident(plwo),
        ],
        out_specs=[tile(zw), tile(d), tile(d)],
        out_shape=[jax.ShapeDtypeStruct((bsz, s, zw), F32), jax.ShapeDtypeStruct((bsz, s, d), BF16),
                   jax.ShapeDtypeStruct((bsz, s, d), F32)],
        scratch_shapes=[pltpu.VMEM((tm + halo_c, cv), F32),
                        pltpu.VMEM((2, SUBLANES - 1, tm + halo_c - SUBLANES, LANES), F32),
                        pltpu.VMEM((tm + halo_p, pw), F32),
                        pltpu.VMEM((tm, cv), F32),
                        pltpu.VMEM((tm, 2 * d), F32)],
        compiler_params=pltpu.CompilerParams(
            dimension_semantics=("parallel", "arbitrary"), vmem_limit_bytes=VMEM_LIMIT_BYTES),
        name="front",
    )(x, ada, wz, wu, wp, wg, cvw, cvb.reshape(1, cv), lng.reshape(1, cv), lnb.reshape(1, cv), cvwo,
      plw, pls.reshape(1, pw), plwo)


def _head_sum(x, head):
    assert LANES == 2 * head
    low = lax.broadcasted_iota(jnp.int32, (1, LANES), 1) < head
    out = []
    for s0 in range(0, x.shape[1], LANES):
        slab = x[:, s0:s0 + LANES]
        tot_low = jnp.sum(jnp.where(low, slab, 0.0), axis=-1, keepdims=True)
        tot_high = jnp.sum(slab, axis=-1, keepdims=True) - tot_low
        out.append(jnp.where(low, tot_low, tot_high))
    return jnp.concatenate(out, axis=1)


def _rwkv_kernel(*refs, tm, rw, head, has_vres):
    if has_vres:
        (z_ref, vf_ref, mu_ref, wa2_ref, g2_ref, v2_ref, vec_ref, yg_ref, zbuf, h_s) = refs
        vfo_ref = None
    else:
        (z_ref, mu_ref, wa2_ref, g2_ref, vec_ref, yg_ref, vfo_ref, zbuf, h_s) = refs
        vf_ref = v2_ref = None
    i = pl.program_id(1)
    gw = HEAD_GROUP * head
    n_grp = rw // gw
    L = CHUNK
    n_chunk = tm // L
    w0, a0, k_k, k_a, r_k, gn_g, gn_b, v0 = (vec_ref[j:j + 1, :] for j in range(8))
    lane = lax.broadcasted_iota(jnp.int32, (1, LANES), 1)

    @pl.when(i == 0)
    def _():
        h_s[...] = jnp.zeros(h_s.shape, F32)
        zbuf[0:SUBLANES, :] = jnp.zeros((SUBLANES, zbuf.shape[1]), F32)

    def prepare(ch):
        rows = ch["rows"]
        p = ch["c"]

        def shifted(c0, c1):
            zr = z_ref[rows, c0:c1]
            if p == 0:
                zbuf[SUBLANES:SUBLANES + L, c0:c1] = zr
                prev = zbuf[SUBLANES - 1:SUBLANES - 1 + L, c0:c1]
            else:
                prev = z_ref[p * L - 1:(p + 1) * L - 1, c0:c1]
            return zr + (prev - zr) * mu_ref[:, c0:c1]

        lwa = shifted(3 * rw, 3 * rw + LANES)
        lora_in = jnp.where(lane < LANES // 2, jnp.tanh(lwa), lwa)
        wl = _dot(lora_in, wa2_ref[...])
        lw_c = -DECAY_SCALE * _sigmoid(w0 + wl[:, 0:rw])
        a = _sigmoid(a0 + wl[:, rw:2 * rw])
        lg = shifted(3 * rw + LANES, 3 * rw + 2 * LANES)
        ch["g"] = _dot(_sigmoid(lg), g2_ref[...])
        v = shifted(2 * rw, 3 * rw)
        if has_vres:
            lv = shifted(3 * rw + 2 * LANES, 3 * rw + 3 * LANES)
            v = v + (vf_ref[rows, :] - v) * _sigmoid(v0 + _dot(lv, v2_ref[...]))
        else:
            vfo_ref[rows, :] = v
        k = shifted(rw, 2 * rw)
        kk = k * k_k
        kkn = kk * jnp.minimum(lax.rsqrt(_head_sum(kk * kk, head)), 1.0 / KK_NORM_FLOOR)
        kp = k * (1.0 + (a - 1.0) * k_a)
        r = shifted(0, rw)
        ch["bonus"] = _head_sum(r * kp * r_k, head) * v
        hi, md, _ = _split3(lw_c)
        dd = lambda a_, b_: jnp.dot(a_, b_, preferred_element_type=F32)
        cl = dd(tri, hi) + dd(tri, md)
        c_mid = cl[mid:mid + 1, :]
        c_end = cl[L - 1:L, :]
        w_inv = jnp.exp(c_mid - cl)
        w_end = jnp.exp(c_end - cl)
        ch["e0"] = jnp.exp(c_mid)
        ch["wl_end"] = jnp.exp(c_end)
        ch["a_t"] = -kkn * jnp.exp(cl - lw_c - c_mid)
        ch["r_t"] = r * jnp.exp(cl - c_mid)
        b_v = kkn * a
        ch["b_t"] = b_v * w_inv
        ch["k_t"] = kp * w_inv
        ch["b_h"] = b_v * w_end
        ch["k_h"] = kp * w_end
        ch["v"] = v

    ri = lax.broadcasted_iota(jnp.int32, (L, L), 0)
    ci = lax.broadcasted_iota(jnp.int32, (L, L), 1)
    tri = jnp.where(ri >= ci, 1.0, 0.0).astype(BF16)
    crow = lax.broadcasted_iota(jnp.int32, (L, gw), 0)
    ccol = lax.broadcasted_iota(jnp.int32, (L, gw), 1) % L
    cmask_s = crow > ccol
    cmask_i = crow >= ccol
    eye_c = crow == ccol
    eye_f = jnp.where(eye_c, 1.0, 0.0)
    glane = lax.broadcasted_iota(jnp.int32, (1, gw), 1)
    head_masks = [(glane >= h * head) & (glane < (h + 1) * head) for h in range(HEAD_GROUP)]
    mid = L // 2 - 1
    n_dbl = int(math.log2(L))

    def stack(x):
        xb = x.astype(BF16)
        zero = jnp.zeros_like(xb)
        return jnp.concatenate([jnp.where(m, xb, zero) for m in head_masks], axis=0)

    def pair_products(ch):
        for gi in range(n_grp):
            ls = slice(gi * gw, (gi + 1) * gw)
            st = ch["grp"][gi]
            a_t, st["r_t"] = ch["a_t"][:, ls], ch["r_t"][:, ls]
            st["e0"], st["wl_end"] = ch["e0"][:, ls], ch["wl_end"][:, ls]
            st["sv"] = stack(ch["v"][:, ls])
            st["sa"] = stack(a_t)
            m4 = _dot_nt(jnp.concatenate([a_t, st["r_t"]], axis=0),
                         jnp.concatenate([stack(ch["b_t"][:, ls]), stack(ch["k_t"][:, ls])], axis=0))
            st["c_ab"] = jnp.where(cmask_s, m4[0:L, 0:gw], 0.0)
            st["c_ak"] = jnp.where(cmask_s, m4[0:L, gw:2 * gw], 0.0)
            st["c_rb"] = jnp.where(cmask_i, m4[L:2 * L, 0:gw], 0.0)
            st["c_rk"] = jnp.where(cmask_i, m4[L:2 * L, gw:2 * gw], 0.0)
            st["ct"] = _dot_nt(eye_f, jnp.concatenate([stack(ch["b_h"][:, ls]), stack(ch["k_h"][:, ls])],
                                                      axis=0))

    def square_and_v(ch):
        for st in ch["grp"]:
            st["s_pow"] = _dot(st["c_ab"], stack(st["c_ab"]))
            st["t_c"] = jnp.where(eye_c, 1.0, st["c_ab"])
            xv = _dot(jnp.concatenate([st["c_ak"], st["c_rk"], st["ct"][:, gw:2 * gw]], axis=0), st["sv"])
            st["v2"] = xv[0:L]
            st["rk_v"] = xv[L:2 * L]
            st["kh_v"] = xv[2 * L:3 * L]

    def doubling(it):
        def run(ch):
            for st in ch["grp"]:
                bd = stack(st["s_pow"])
                if it < n_dbl - 1:
                    prod = _dot(jnp.concatenate([st["t_c"], st["s_pow"]], axis=0), bd)
                    st["t_c"] = st["t_c"] + prod[0:L]
                    st["s_pow"] = prod[L:2 * L]
                else:
                    st["t_c"] = st["t_c"] + _dot(st["t_c"], bd)
        return run

    def solve(ch):
        for st in ch["grp"]:
            x2 = _dot(st["t_c"], jnp.concatenate([st["sa"], stack(st["v2"])], axis=1))
            st["s_x2"] = jnp.concatenate([stack(x2[:, 0:gw]), stack(x2[:, gw:2 * gw])], axis=1)

    def transfer(ch):
        for st in ch["grp"]:
            wp = _dot(jnp.concatenate([st["c_rb"], st["ct"][:, 0:gw]], axis=0), st["s_x2"])
            w1, pq = wp[0:L], wp[L:2 * L]
            rhat = (st["r_t"] + w1[:, 0:gw]) * st["e0"]
            p_c = jnp.where(eye_c, st["wl_end"], 0.0) + pq[:, 0:gw] * st["e0"]
            st["rp"] = jnp.concatenate([rhat, p_c], axis=0)
            st["y0"] = w1[:, gw:2 * gw] + st["rk_v"]
            st["q_c"] = pq[:, gw:2 * gw] + st["kh_v"]

    def state_pass(ch):
        ys = []
        for gi, st in enumerate(ch["grp"]):
            yh = _dot(st["rp"], stack(h_s[gi]))
            ys.append(yh[0:L] + st["y0"])
            h_s[gi] = yh[L:2 * L] + st["q_c"]
        ch["y"] = jnp.concatenate(ys, axis=1)

    def finish(ch):
        y = ch["y"]
        mean = _head_sum(y, head) * (1.0 / head)
        dlt = y - mean
        var = _head_sum(dlt * dlt, head) * (1.0 / head)
        yn = dlt * lax.rsqrt(var + RW_GN_EPS) * gn_g + gn_b
        yg_ref[ch["rows"], :] = ((yn + ch["bonus"]) * ch["g"]).astype(yg_ref.dtype)

    stages = [prepare, pair_products, square_and_v] + [doubling(it) for it in range(1, n_dbl)]
    stages += [solve, transfer, state_pass, finish]
    chunks = [dict(c=c, rows=slice(c * L, (c + 1) * L), grp=[dict() for _ in range(n_grp)])
              for c in range(n_chunk)]
    for step in range(n_chunk + len(stages) - 1):
        for ch in chunks:
            if 0 <= step - ch["c"] < len(stages):
                stages[step - ch["c"]](ch)

    zbuf[0:SUBLANES, :] = z_ref[tm - SUBLANES:tm, :]


def _rwkv_call(z, vfirst, mu, wa2, g2, v2, vecs, tm, head):
    bsz, s, zw = z.shape
    rw = g2.shape[1]
    n_tiles = s // tm
    has_vres = vfirst is not None
    c2 = lambda b, i: (0, 0)
    tile = lambda w: pl.BlockSpec((None, tm, w), lambda b, i: (b, i, 0))
    in_specs = [tile(zw)]
    args = [z]
    if has_vres:
        in_specs.append(tile(rw))
        args.append(vfirst)
    in_specs += [pl.BlockSpec(mu.shape, c2), pl.BlockSpec(wa2.shape, c2), pl.BlockSpec(g2.shape, c2)]
    args += [mu, wa2, g2]
    if has_vres:
        in_specs.append(pl.BlockSpec(v2.shape, c2))
        args.append(v2)
    in_specs.append(pl.BlockSpec(vecs.shape, c2))
    args.append(vecs)
    out_specs = [tile(rw)]
    out_shape = [jax.ShapeDtypeStruct((bsz, s, rw), BF16)]
    if not has_vres:
        out_specs.append(tile(rw))
        out_shape.append(jax.ShapeDtypeStruct((bsz, s, rw), F32))
    gw = HEAD_GROUP * head
    scratch = [pltpu.VMEM((CHUNK + SUBLANES, zw), F32),
               pltpu.VMEM((rw // gw, CHUNK, gw), F32)]
    kern = functools.partial(_rwkv_kernel, tm=tm, rw=rw, head=head, has_vres=has_vres)
    res = pl.pallas_call(
        kern,
        grid=(bsz, n_tiles),
        in_specs=in_specs,
        out_specs=out_specs,
        out_shape=out_shape,
        scratch_shapes=scratch,
        compiler_params=pltpu.CompilerParams(
            dimension_semantics=("parallel", "arbitrary"), vmem_limit_bytes=VMEM_LIMIT_BYTES),
        name="rwkv_vres" if has_vres else "rwkv",
    )(*args)
    return (res[0], vfirst) if has_vres else (res[0], res[1])


def _back_kernel(yg_ref, g0_ref, m12_ref, x_ref, ada_ref, wo_ref, wout_ref, lnm_ref, w1_ref, w2_ref, lnf_ref,
                 o_ref, *, alpha, n_split):
    tm = x_ref.shape[0]
    halves = [slice(0, tm // 2), slice(tm // 2, tm)]
    sh = ada_ref[3:4, :]
    sc = ada_ref[4:5, :]
    gt = ada_ref[5:6, :]
    dff = w1_ref.shape[1]
    step = dff // n_split
    relu2 = lambda t: jnp.square(jnp.maximum(t, 0.0)).astype(BF16)

    y_rw = [jnp.dot(yg_ref[rs, :], wo_ref[...], preferred_element_type=F32) for rs in halves]
    outs = [_dot(_sigmoid(g0_ref[rs, :].astype(F32)) * y + m12_ref[rs, :], wout_ref[...])
            for rs, y in zip(halves, y_rw)]
    xs, hs, t0 = [], [], []
    for rs, o in zip(halves, outs):
        x1 = _layernorm(alpha * x_ref[rs, :] + ada_ref[2:3, :] * o, lnm_ref[0:1, :], lnm_ref[1:2, :], LN_EPS)
        xs.append(x1)
        hs.append((x1 * (1.0 + sc) + sh).astype(BF16))
        t0.append(relu2(jnp.dot(hs[-1], w1_ref[:, 0:step], preferred_element_type=F32)))
    h = jnp.concatenate(hs, axis=0)
    acc = jnp.dot(jnp.concatenate(t0, axis=0), w2_ref[0:step, :], preferred_element_type=F32)
    for j in range(1, n_split - 1):
        t = relu2(jnp.dot(h, w1_ref[:, j * step:(j + 1) * step], preferred_element_type=F32))
        acc = acc + jnp.dot(t, w2_ref[j * step:(j + 1) * step, :], preferred_element_type=F32)
    t = relu2(jnp.dot(h, w1_ref[:, dff - step:dff], preferred_element_type=F32))
    for rs, x1 in zip(halves, xs):
        part = jnp.dot(t[rs, :], w2_ref[dff - step:dff, :], preferred_element_type=F32)
        o_ref[rs, :] = _layernorm(alpha * x1 + gt * (acc[rs, :] + part), lnf_ref[0:1, :], lnf_ref[1:2, :], LN_EPS)


def _back_call(yg, g0, m12, x, ada, wo, wout, lnm, w1, w2, lnf, tm, alpha):
    bsz, s, d = x.shape
    dff = w1.shape[1]
    c2 = lambda b, i: (0, 0)
    tile = lambda w: pl.BlockSpec((None, tm, w), lambda b, i: (b, i, 0))
    resident = lambda a: pl.BlockSpec(a.shape, c2, pipeline_mode=pl.Buffered(1))
    kern = functools.partial(_back_kernel, alpha=alpha, n_split=dff // d)
    return pl.pallas_call(
        kern,
        grid=(bsz, s // tm),
        in_specs=[
            tile(yg.shape[-1]), tile(d), tile(d), tile(d),
            pl.BlockSpec((None,) + ada.shape[1:], lambda b, i: (b, 0, 0)),
            resident(wo), resident(wout), pl.BlockSpec(lnm.shape, c2),
            resident(w1), resident(w2), pl.BlockSpec(lnf.shape, c2),
        ],
        out_specs=tile(d),
        out_shape=jax.ShapeDtypeStruct((bsz, s, d), F32),
        compiler_params=pltpu.CompilerParams(
            dimension_semantics=("parallel", "arbitrary"), vmem_limit_bytes=VMEM_LIMIT_BYTES),
        name="back",
    )(yg, g0, m12, x, ada, wo, wout, lnm, w1, w2, lnf)


def _pick_tile(s, pref):
    tm = min(pref, s)
    assert s % tm == 0 and tm % CHUNK == 0, (s, tm)
    return tm


def kernel(x, c, ada_w, ada_b, w_in, w_in_vres, shift_mu, shift_mu_vres, rw_w0, rw_w2, rw_a0, rw_a2, rw_g2, rw_v0, rw_v2, rw_kk, rw_ka, rw_rk, rw_gn_g, rw_gn_b, rw_wo, cv_w, cv_b, cv_ln_g, cv_ln_b, cv_wo, pl_w, pl_scale, pl_wo, w_out, ln_m_g, ln_m_b, mlp_w1, mlp_w2, ln_f_g, ln_f_b):
    bsz, s, d = x.shape
    depth = ada_w.shape[0]
    heads, head = rw_rk.shape[1], rw_rk.shape[2]
    rw = heads * head
    lw_w, la_w, lg_w = rw_w2.shape[1], rw_a2.shape[1], rw_g2.shape[1]
    lv_w = rw_v2.shape[1]
    cv = cv_w.shape[2]
    pw = pl_scale.shape[1]
    assert lw_w + la_w == LANES and lg_w == LANES and lv_w <= LANES and LANES % head == 0
    assert rw % (HEAD_GROUP * head) == 0 and head == CHUNK
    alpha = float((2 * depth) ** 0.25)
    rw_shift = 3 * rw + lw_w + la_w + lg_w
    cv_off, pl_off = rw_shift, rw_shift + 2 * cv
    gt_off = pl_off + pw

    tm_in = _pick_tile(s, 512)
    tm_rw = _pick_tile(s, 1024)
    tm_ff = _pick_tile(s, 512)

    ada = _ada_call(c, ada_w, ada_b)

    v_first = None
    for l in range(depth):
        has_vres = l > 0
        w_l = w_in[l]
        wz = w_l[:, :rw_shift]
        mu = shift_mu[l]
        if has_vres:
            pad = LANES - lv_w
            wz = jnp.concatenate([wz, w_in_vres[l - 1], jnp.zeros((d, pad), F32)], axis=1)
            mu = jnp.concatenate([mu, shift_mu_vres[l - 1], jnp.zeros((pad,), F32)])
        wu = w_l[:, cv_off:pl_off].reshape(d, 2, cv // LANES, LANES).transpose(0, 2, 1, 3).reshape(d, 2 * cv)
        wz, wu, wp, wg = (t.astype(BF16) for t in (wz, wu, w_l[:, pl_off:gt_off], w_l[:, gt_off:]))
        z, g, m12 = _front_call(x, ada[l], wz, wu, wp, wg, cv_w[l], cv_b[l], cv_ln_g[l], cv_ln_b[l],
                                cv_wo[l].astype(BF16), pl_w[l].astype(BF16), pl_scale[l], pl_wo[l].astype(BF16),
                                tm_in)

        wa2 = jnp.zeros((LANES, 2 * rw), F32)
        wa2 = wa2.at[:lw_w, :rw].set(rw_w2[l]).at[lw_w:, rw:].set(rw_a2[l]).astype(BF16)
        v0 = rw_v0[l - 1] if has_vres else jnp.zeros((rw,), F32)
        vecs = jnp.stack([rw_w0[l], rw_a0[l], rw_kk[l], rw_ka[l], rw_rk[l].reshape(rw),
                          rw_gn_g[l], rw_gn_b[l], v0])
        v2 = None
        if has_vres:
            v2 = jnp.zeros((LANES, rw), F32).at[:lv_w].set(rw_v2[l - 1]).astype(BF16)
        yg, v_first = _rwkv_call(z, v_first, mu.reshape(1, -1), wa2, rw_g2[l].astype(BF16), v2, vecs, tm_rw, head)

        lnm = jnp.stack([ln_m_g[l], ln_m_b[l]])
        lnf = jnp.stack([ln_f_g[l], ln_f_b[l]])
        x = _back_call(yg, g, m12, x, ada[l], rw_wo[l].astype(BF16), w_out[l].astype(BF16), lnm,
                       mlp_w1[l].astype(BF16), mlp_w2[l].astype(BF16), lnf, tm_ff, alpha)
    return x
```
